```python
import math
import jax, jax.numpy as jnp
from jax import lax
import numpy as np

D_MODEL = 1024
BATCH = 2
SEQ = 16384
DEPTH = 4

N_EVEN = (DEPTH + 1) // 2
N_ODD = DEPTH // 2
EPS = 1e-6
ROPE_THETA = 10000.0
BLOCK = 128

A_HEADS = 8
A_KV_HEADS = 2
A_GROUP = A_HEADS // A_KV_HEADS
A_HEAD_DIM = D_MODEL // 16
A_WINDOW = 128
A_WIDTH = A_HEADS * A_HEAD_DIM
A_KV_WIDTH = A_KV_HEADS * A_HEAD_DIM
B_WIDTH = D_MODEL // 2
POOL_WINDOWS = (2, 4, 8, 16)
B_GROUPS = len(POOL_WINDOWS)
B_GROUP_DIM = B_WIDTH // B_GROUPS
EVEN_IN = A_WIDTH + 2 * A_KV_WIDTH + B_WIDTH
C_HEADS = 8
C_NOPE = 64
C_ROPE = 32
C_VDIM = 64
C_Q_RANK = D_MODEL // 4
C_KV_RANK = D_MODEL // 8
C_WIDTH = C_HEADS * C_VDIM
D_WIDTH = D_MODEL // 2
D_BLOCKS = 8
D_BLOCK_DIM = D_WIDTH // D_BLOCKS
CONV_WIDTH = 4
LRU_C = 8.0
ODD_IN = C_Q_RANK + C_KV_RANK + C_ROPE + 2 * D_WIDTH
D_FF = 4 * D_MODEL

kernel_name = "hybrid_bidir_swa_pool_mla_rglru"

F32 = jnp.float32


def rmsnorm(x, g):
    xf = x.astype(F32)
    y = xf * lax.rsqrt(jnp.mean(xf * xf, axis=-1, keepdims=True) + EPS)
    return (y * g.astype(F32)).astype(x.dtype)


def rope(x):
    S, d = x.shape[1], x.shape[-1]
    half = d // 2
    inv = ROPE_THETA ** (-jnp.arange(half, dtype=F32) / half)
    ang = jnp.arange(S, dtype=F32)[:, None] * inv[None, :]
    shape = (1, S) + (1,) * (x.ndim - 3) + (half,)
    cos = jnp.cos(ang).reshape(shape)
    sin = jnp.sin(ang).reshape(shape)
    xf = x.astype(F32)
    x1, x2 = xf[..., :half], xf[..., half:]
    return jnp.concatenate([x1 * cos - x2 * sin, x2 * cos + x1 * sin], axis=-1).astype(x.dtype)


def windowed_gqa(q, k, v, sink):
    Bsz, S = q.shape[0], q.shape[1]
    nb = S // BLOCK
    qb = q.reshape(Bsz, nb, BLOCK, A_KV_HEADS, A_GROUP, A_HEAD_DIM)

    def band(t):
        tp = jnp.pad(t, ((0, 0), (BLOCK, BLOCK), (0, 0), (0, 0)))
        tp = tp.reshape(Bsz, nb + 2, BLOCK, A_KV_HEADS, A_HEAD_DIM)
        return jnp.concatenate([tp[:, :-2], tp[:, 1:-1], tp[:, 2:]], axis=2)

    kb, vb = band(k), band(v)
    s = jnp.einsum('bnqhgd,bnjhd->bnhgqj', qb, kb).astype(F32) * (A_HEAD_DIM ** -0.5)
    blk = jnp.arange(nb)[:, None, None]
    qpos = blk * BLOCK + jnp.arange(BLOCK)[None, :, None]
    kpos = (blk - 1) * BLOCK + jnp.arange(3 * BLOCK)[None, None, :]
    valid = (jnp.abs(kpos - qpos) <= A_WINDOW) & (kpos >= 0) & (kpos < S)
    s = jnp.where(valid[None, :, None, None], s, -jnp.inf)
    sk = sink.astype(F32).reshape(1, 1, A_KV_HEADS, A_GROUP, 1, 1)
    m = jnp.maximum(jnp.max(s, axis=-1, keepdims=True), sk)
    p = jnp.exp(s - m)
    p = p / (jnp.sum(p, axis=-1, keepdims=True) + jnp.exp(sk - m))
    out = jnp.einsum('bnhgqj,bnjhd->bnqhgd', p.astype(v.dtype), vb)
    return out.reshape(Bsz, S, A_WIDTH)


def multiscale_pool(u, w_pool, pool_scale):
    Bsz, S = u.shape[0], u.shape[1]
    uf = u.astype(F32).reshape(Bsz, S, B_GROUPS, B_GROUP_DIM)
    cs = jnp.pad(jnp.cumsum(uf, axis=1), ((0, 0), (1, 0), (0, 0), (0, 0)))
    t = jnp.arange(S)
    outs = []
    for g, w in enumerate(POOL_WINDOWS):
        half = w // 2
        lo = jnp.clip(t - half, 0, S)
        hi = jnp.clip(t + half, 0, S)
        win_sum = cs[:, hi, g] - cs[:, lo, g]
        cnt = (hi - lo).astype(F32)[None, :, None]
        outs.append(win_sum / cnt - uf[:, :, g])
    d = jnp.stack(outs, axis=2)
    y = jnp.einsum('bsgi,gij->bsgj', d, w_pool.astype(F32)).reshape(Bsz, S, B_WIDTH)
    return (y * pool_scale.astype(F32)).astype(u.dtype)


def dense_mla(qn, qr, kn, kr, v):
    Bsz, S = qn.shape[0], qn.shape[1]
    nb = S // BLOCK
    scale = (C_NOPE + C_ROPE) ** -0.5

    def blocks(t):
        return jnp.moveaxis(t.reshape((Bsz, nb, BLOCK) + t.shape[2:]), 1, 0)

    def one(args):
        qn_b, qr_b = args
        s = (jnp.einsum('bqhd,bkhd->bhqk', qn_b, kn) + jnp.einsum('bqhr,bkr->bhqk', qr_b, kr)).astype(F32) * scale
        p = jax.nn.softmax(s, axis=-1).astype(v.dtype)
        return jnp.einsum('bhqk,bkhd->bqhd', p, v)

    out = lax.map(one, (blocks(qn), blocks(qr)))
    return jnp.moveaxis(out, 0, 1).reshape(Bsz, S, C_WIDTH)


def _lin_combine(left, right):
    a1, b1 = left
    a2, b2 = right
    return a1 * a2, a2 * b1 + b2


def rglru_block(xr, xg, conv_w, conv_b, wa, ba, wx, bx, lam):
    Bsz, S = xr.shape[0], xr.shape[1]
    left = CONV_WIDTH // 2
    xp = jnp.pad(xr, ((0, 0), (left, CONV_WIDTH - 1 - left), (0, 0)))
    xc = conv_b + conv_w[0] * xp[:, 0:S]
    for j in range(1, CONV_WIDTH):
        xc = xc + conv_w[j] * xp[:, j:j + S]
    xcf = xc.astype(F32)
    xblk = xcf.reshape(Bsz, S, D_BLOCKS, D_BLOCK_DIM)
    hs = []
    for dirn in range(2):
        r = jax.nn.sigmoid(jnp.einsum('bsni,nij->bsnj', xblk, wa[dirn].astype(F32)).reshape(Bsz, S, D_WIDTH) + ba[dirn].astype(F32))
        i = jax.nn.sigmoid(jnp.einsum('bsni,nij->bsnj', xblk, wx[dirn].astype(F32)).reshape(Bsz, S, D_WIDTH) + bx[dirn].astype(F32))
        log_a = -LRU_C * r * jax.nn.softplus(-lam[dirn].astype(F32))
        a = jnp.exp(log_a)
        b = jnp.sqrt(-jnp.expm1(2.0 * log_a)) * (i * xcf)
        _, h = lax.associative_scan(_lin_combine, (a, b), axis=1, reverse=(dirn == 1))
        hs.append(h)
    y = (hs[0] + hs[1]) * jax.nn.gelu(xg.astype(F32))
    return y.astype(xr.dtype)


def even_mixer(h, w_in, sink, w_pool, pool_scale, w_out):
    Bsz, S = h.shape[0], h.shape[1]
    z = h @ w_in
    q, k, v, u = jnp.split(z, [A_WIDTH, A_WIDTH + A_KV_WIDTH, A_WIDTH + 2 * A_KV_WIDTH], axis=-1)
    q = rope(q.reshape(Bsz, S, A_HEADS, A_HEAD_DIM))
    k = rope(k.reshape(Bsz, S, A_KV_HEADS, A_HEAD_DIM))
    v = v.reshape(Bsz, S, A_KV_HEADS, A_HEAD_DIM)
    ya = windowed_gqa(q, k, v, sink)
    yb = multiscale_pool(u, w_pool, pool_scale)
    return jnp.concatenate([ya, yb], axis=-1) @ w_out


def odd_mixer(h, w_in, g_cq, w_uq, g_ckv, w_ukv, conv_w, conv_b, wa, ba, wx, bx, lam, w_out):
    Bsz, S = h.shape[0], h.shape[1]
    z = h @ w_in
    i1 = C_Q_RANK
    i2 = i1 + C_KV_RANK
    i3 = i2 + C_ROPE
    i4 = i3 + D_WIDTH
    cq, ckv, kr, xr, xg = jnp.split(z, [i1, i2, i3, i4], axis=-1)
    q = (rmsnorm(cq, g_cq) @ w_uq).reshape(Bsz, S, C_HEADS, C_NOPE + C_ROPE)
    qn, qr = q[..., :C_NOPE], rope(q[..., C_NOPE:])
    kv = (rmsnorm(ckv, g_ckv) @ w_ukv).reshape(Bsz, S, C_HEADS, C_NOPE + C_VDIM)
    kn, v = kv[..., :C_NOPE], kv[..., C_NOPE:]
    yc = dense_mla(qn, qr, kn, rope(kr), v)
    yd = rglru_block(xr, xg, conv_w, conv_b, wa, ba, wx, bx, lam)
    return jnp.concatenate([yc, yd], axis=-1) @ w_out


def sq_relu_mlp(h, w1, w2):
    u = jax.nn.relu(h @ w1)
    return (u * u) @ w2


def setup_inputs(seed: int = 0) -> dict:
    key = jax.random.key(seed)
    ks = iter(jax.random.split(key, 40))

    def nrm(shape, fan_in):
        return jax.random.normal(next(ks), shape, F32) * (fan_in ** -0.5)

    def gain(shape):
        return 1.0 + 0.02 * jax.random.normal(next(ks), shape, F32)

    def bias(shape):
        return 0.01 * jax.random.normal(next(ks), shape, F32)

    x = jax.random.normal(next(ks), (BATCH, SEQ, D_MODEL), F32)
    u = jax.random.uniform(next(ks), (N_ODD, 2, D_WIDTH), F32, 0.9, 0.999)
    a0 = u ** (1.0 / LRU_C)
    lam = jnp.log(a0) - jnp.log1p(-a0)
    return {
        "x": x,
        "e_norm_mix": gain((N_EVEN, D_MODEL)),
        "e_w_in": nrm((N_EVEN, D_MODEL, EVEN_IN), D_MODEL),
        "e_sink": 0.5 * jax.random.normal(next(ks), (N_EVEN, A_HEADS), F32),
        "e_w_pool": nrm((N_EVEN, B_GROUPS, B_GROUP_DIM, B_GROUP_DIM), B_GROUP_DIM),
        "e_pool_scale": gain((N_EVEN, B_WIDTH)),
        "e_w_out": nrm((N_EVEN, D_MODEL, D_MODEL), D_MODEL),
        "o_norm_mix": gain((N_ODD, D_MODEL)),
        "o_w_in": nrm((N_ODD, D_MODEL, ODD_IN), D_MODEL),
        "o_g_cq": gain((N_ODD, C_Q_RANK)),
        "o_w_uq": nrm((N_ODD, C_Q_RANK, C_HEADS * (C_NOPE + C_ROPE)), C_Q_RANK),
        "o_g_ckv": gain((N_ODD, C_KV_RANK)),
        "o_w_ukv": nrm((N_ODD, C_KV_RANK, C_HEADS * (C_NOPE + C_VDIM)), C_KV_RANK),
        "o_conv_w": nrm((N_ODD, CONV_WIDTH, D_WIDTH), CONV_WIDTH),
        "o_conv_b": bias((N_ODD, D_WIDTH)),
        "o_lru_wa": nrm((N_ODD, 2, D_BLOCKS, D_BLOCK_DIM, D_BLOCK_DIM), D_BLOCK_DIM),
        "o_lru_ba": bias((N_ODD, 2, D_WIDTH)),
        "o_lru_wx": nrm((N_ODD, 2, D_BLOCKS, D_BLOCK_DIM, D_BLOCK_DIM), D_BLOCK_DIM),
        "o_lru_bx": bias((N_ODD, 2, D_WIDTH)),
        "o_lru_lambda": lam,
        "o_w_out": nrm((N_ODD, D_MODEL, D_MODEL), D_MODEL),
        "norm_mlp": gain((DEPTH, D_MODEL)),
        "w_mlp1": nrm((DEPTH, D_MODEL, D_FF), D_MODEL),
        "w_mlp2": nrm((DEPTH, D_FF, D_MODEL), D_FF),
        "final_norm": gain((D_MODEL,)),
    }


def reference(x, e_norm_mix, e_w_in, e_sink, e_w_pool, e_pool_scale, e_w_out,
              o_norm_mix, o_w_in, o_g_cq, o_w_uq, o_g_ckv, o_w_ukv, o_conv_w, o_conv_b,
              o_lru_wa, o_lru_ba, o_lru_wx, o_lru_bx, o_lru_lambda, o_w_out,
              norm_mlp, w_mlp1, w_mlp2, final_norm):
    for layer in range(DEPTH):
        if layer % 2 == 0:
            e = layer // 2
            h = rmsnorm(x, e_norm_mix[e])
            x = x + even_mixer(h, e_w_in[e], e_sink[e], e_w_pool[e], e_pool_scale[e], e_w_out[e])
        else:
            o = layer // 2
            h = rmsnorm(x, o_norm_mix[o])
            x = x + odd_mixer(h, o_w_in[o], o_g_cq[o], o_w_uq[o], o_g_ckv[o], o_w_ukv[o],
                              o_conv_w[o], o_conv_b[o], o_lru_wa[o], o_lru_ba[o], o_lru_wx[o],
                              o_lru_bx[o], o_lru_lambda[o], o_w_out[o])
        x = x + sq_relu_mlp(rmsnorm(x, norm_mlp[layer]), w_mlp1[layer], w_mlp2[layer])
    return rmsnorm(x, final_norm)
```

```python
import functools
import math

import jax
import jax.numpy as jnp
from jax import lax
from jax.experimental import pallas as pl
from jax.experimental.pallas import tpu as pltpu

F32 = jnp.float32
BF16 = jnp.bfloat16

EPS = 1e-6
ROPE_THETA = 10000.0
LANES = 128
SUBLANES = 8
VMEM_LIMIT = 48 * 1024 * 1024

A_HEADS = 8
A_KV_HEADS = 2
A_HEAD_DIM = 64
A_WINDOW = 128
A_WIDTH = A_HEADS * A_HEAD_DIM
POOL_WINDOWS = (2, 4, 8, 16)
B_GROUP_DIM = 128
B_WIDTH = B_GROUP_DIM * len(POOL_WINDOWS)
POOL_HALO = 8
C_HEADS = 8
C_NOPE = 64
C_ROPE = 32
C_VDIM = 64
C_Q_RANK = 256
C_KV_RANK = 128
D_WIDTH = 512
D_BLOCKS = 8
D_BLOCK_DIM = 64
CONV_WIDTH = 4
LRU_C = 8.0

NEG_BIG = -1e30
LOG2E = math.log2(math.e)


def _params(*sem):
    return pltpu.CompilerParams(dimension_semantics=sem, vmem_limit_bytes=VMEM_LIMIT)


def _rms(x, g):
    return x * lax.rsqrt(jnp.mean(x * x, axis=-1, keepdims=True) + EPS) * g


def _rope_slab(xs, cos, sin, half):
    lane = lax.broadcasted_iota(jnp.int32, (1, LANES), 1)
    first = (lane % (2 * half)) < half
    swapped = jnp.where(first, pltpu.roll(xs, LANES - half, 1), pltpu.roll(xs, half, 1))
    return xs * cos + swapped * sin


def _even_prep_kernel(x_ref, g_ref, w_ref, cos_ref, sin_ref, q_ref, k_ref, v_ref, u_ref):
    h = _rms(x_ref[...], g_ref[...]).astype(BF16)
    z = jnp.dot(h, w_ref[...], preferred_element_type=F32)
    cos = cos_ref[...]
    sin = sin_ref[...]
    n_q = A_WIDTH // LANES
    for c in range(n_q):
        r = _rope_slab(z[:, c * LANES:(c + 1) * LANES], cos, sin, A_HEAD_DIM // 2)
        q_ref[:, c * LANES:(c + 1) * LANES] = (r * (A_HEAD_DIM ** -0.5)).astype(BF16)
    for c in range(A_KV_HEADS):
        lo = A_WIDTH + c * LANES
        k_ref[:, c * LANES:(c + 1) * LANES] = _rope_slab(z[:, lo:lo + LANES], cos, sin, A_HEAD_DIM // 2).astype(BF16)
    lo = A_WIDTH + A_KV_HEADS * LANES
    v_ref[...] = z[:, lo:lo + A_KV_HEADS * LANES].astype(BF16)
    u_ref[...] = z[:, lo + A_KV_HEADS * LANES:]


def _even_prep(x2, g, w, cos, sin, seq, tm):
    T, D = x2.shape
    n_in = w.shape[1]
    kvw = A_KV_HEADS * LANES
    nt_seq = seq // tm
    tok = lambda i: (i, 0)
    tab = lambda i: (i % nt_seq, 0)
    fixed = lambda i: (0, 0)
    return pl.pallas_call(
        _even_prep_kernel,
        grid=(T // tm,),
        in_specs=[pl.BlockSpec((tm, D), tok), pl.BlockSpec((1, D), fixed), pl.BlockSpec((D, n_in), fixed),
                  pl.BlockSpec((tm, LANES), tab), pl.BlockSpec((tm, LANES), tab)],
        out_specs=[pl.BlockSpec((tm, A_WIDTH), tok), pl.BlockSpec((tm, kvw), tok),
                   pl.BlockSpec((tm, kvw), tok), pl.BlockSpec((tm, B_WIDTH), tok)],
        out_shape=[jax.ShapeDtypeStruct((T, A_WIDTH), BF16), jax.ShapeDtypeStruct((T, kvw), BF16),
                   jax.ShapeDtypeStruct((T, kvw), BF16), jax.ShapeDtypeStruct((T, B_WIDTH), F32)],
        compiler_params=_params("parallel"),
        name="even_prep",
    )(x2, g, w, cos, sin)


def _even_mix_kernel(sink_ref, x_ref, q_ref, kp_ref, kc_ref, kn_ref, vp_ref, vc_ref, vn_ref,
                     up_ref, uc_ref, un_ref, wpool_ref, pscale_ref, wout_ref, o_ref, uext_ref, *, seq, tq):
    i = pl.program_id(0)
    nt_seq = seq // tq
    pos0 = (i % nt_seq) * tq
    at_start = pos0 == 0
    at_end = pos0 + tq == seq
    nk = tq + 2 * A_WINDOW

    kd = jnp.concatenate([kp_ref[...], kc_ref[...], kn_ref[...]], axis=0)
    vd = jnp.concatenate([vp_ref[...], vc_ref[...], vn_ref[...]], axis=0)
    r = lax.broadcasted_iota(jnp.int32, (tq, nk), 0)
    c = lax.broadcasted_iota(jnp.int32, (tq, nk), 1)
    kpos = pos0 - A_WINDOW + c
    valid = (jnp.abs(c - A_WINDOW - r) <= A_WINDOW) & (kpos >= 0) & (kpos < seq)
    valid2 = jnp.concatenate([valid, valid], axis=0)
    lane = lax.broadcasted_iota(jnp.int32, (1, LANES), 1)
    low = lane < A_HEAD_DIM
    row2 = lax.broadcasted_iota(jnp.int32, (2 * tq, 1), 0)
    ya = []
    for j in range(A_HEADS // 2):
        g = (2 * j) // (A_HEADS // A_KV_HEADS)
        qp = q_ref[:, j * LANES:(j + 1) * LANES]
        zero = jnp.zeros_like(qp)
        lhs = jnp.concatenate([jnp.where(low, qp, zero), jnp.where(low, zero, qp)], axis=0)
        s = lax.dot_general(lhs, kd[:, g * LANES:(g + 1) * LANES], (((1,), (1,)), ((), ())),
                            preferred_element_type=F32)
        s = jnp.where(valid2, s, NEG_BIG)
        sk = jnp.where(row2 < tq, sink_ref[2 * j], sink_ref[2 * j + 1])
        m = jnp.maximum(jnp.max(s, axis=1, keepdims=True), sk)
        p = jnp.exp(s - m)
        den = jnp.sum(p, axis=1, keepdims=True) + jnp.exp(sk - m)
        o = jnp.dot(p.astype(BF16), vd[:, g * LANES:(g + 1) * LANES], preferred_element_type=F32) / den
        ya.append(jnp.where(low, o[:tq], o[tq:]))

    uext_ref[0:POOL_HALO, :] = jnp.where(at_start, 0.0, up_ref[...])
    uext_ref[POOL_HALO:POOL_HALO + tq, :] = uc_ref[...]
    uext_ref[POOL_HALO + tq:, :] = jnp.where(at_end, 0.0, un_ref[...])
    t = pos0 + lax.broadcasted_iota(jnp.int32, (tq, 1), 0)
    yb = []
    for gi, w in enumerate(POOL_WINDOWS):
        half = w // 2
        cols = slice(gi * B_GROUP_DIM, (gi + 1) * B_GROUP_DIM)
        win = uext_ref[POOL_HALO - half:POOL_HALO - half + tq, cols]
        for off in range(-half + 1, half):
            win = win + uext_ref[POOL_HALO + off:POOL_HALO + off + tq, cols]
        cnt = (jnp.minimum(t + half, seq) - jnp.maximum(t - half, 0)).astype(F32)
        d = win / cnt - uc_ref[:, cols]
        y = jnp.dot(d.astype(BF16), wpool_ref[gi], preferred_element_type=F32)
        yb.append(y * pscale_ref[:, cols])

    ycat = jnp.concatenate(ya + yb, axis=1).astype(BF16)
    o_ref[...] = x_ref[...] + jnp.dot(ycat, wout_ref[...], preferred_element_type=F32)


def _even_mix(x2, q, k, v, u, sink, wpool, pscale, wout, seq, tq):
    T, D = x2.shape
    kvw = k.shape[1]
    nt_seq = seq // tq
    wb = tq // A_WINDOW
    hb = tq // POOL_HALO
    n_wb = T // A_WINDOW
    n_hb = T // POOL_HALO
    tok = lambda i: (i, 0)
    prev_w = lambda i: (jnp.maximum(i * wb - 1, 0), 0)
    next_w = lambda i: (jnp.minimum((i + 1) * wb, n_wb - 1), 0)
    prev_h = lambda i: (jnp.maximum(i * hb - 1, 0), 0)
    next_h = lambda i: (jnp.minimum((i + 1) * hb, n_hb - 1), 0)
    fixed2 = lambda i: (0, 0)
    fixed3 = lambda i: (0, 0, 0)
    return pl.pallas_call(
        functools.partial(_even_mix_kernel, seq=seq, tq=tq),
        grid=(T // tq,),
        in_specs=[pl.BlockSpec(memory_space=pltpu.SMEM),
                  pl.BlockSpec((tq, D), tok),
                  pl.BlockSpec((tq, A_WIDTH), tok),
                  pl.BlockSpec((A_WINDOW, kvw), prev_w), pl.BlockSpec((tq, kvw), tok), pl.BlockSpec((A_WINDOW, kvw), next_w),
                  pl.BlockSpec((A_WINDOW, kvw), prev_w), pl.BlockSpec((tq, kvw), tok), pl.BlockSpec((A_WINDOW, kvw), next_w),
                  pl.BlockSpec((POOL_HALO, B_WIDTH), prev_h), pl.BlockSpec((tq, B_WIDTH), tok), pl.BlockSpec((POOL_HALO, B_WIDTH), next_h),
                  pl.BlockSpec(wpool.shape, fixed3), pl.BlockSpec((1, B_WIDTH), fixed2), pl.BlockSpec((D, D), fixed2)],
        out_specs=pl.BlockSpec((tq, D), tok),
        out_shape=jax.ShapeDtypeStruct((T, D), F32),
        scratch_shapes=[pltpu.VMEM((tq + 2 * POOL_HALO, B_WIDTH), F32)],
        compiler_params=_params("parallel"),
        name="even_mix",
    )(sink, x2, q, k, k, k, v, v, v, u, u, u, wpool, pscale, wout)


def _mlp_kernel(x_ref, g_ref, w1_ref, w2_ref, gf_ref, o_ref, h_ref, *, final_norm):
    f = pl.program_id(1)

    @pl.when(f == 0)
    def _():
        x = x_ref[...]
        h_ref[...] = _rms(x, g_ref[...]).astype(BF16)
        o_ref[...] = x

    u = jnp.maximum(jnp.dot(h_ref[...], w1_ref[...], preferred_element_type=F32), 0.0)
    o_ref[...] += jnp.dot((u * u).astype(BF16), w2_ref[...], preferred_element_type=F32)

    if final_norm:
        @pl.when(f == pl.num_programs(1) - 1)
        def _():
            o_ref[...] = _rms(o_ref[...], gf_ref[...])


def _mlp(x2, g, w1, w2, gf, tm, tf, final_norm):
    T, D = x2.shape
    F = w1.shape[1]
    return pl.pallas_call(
        functools.partial(_mlp_kernel, final_norm=final_norm),
        grid=(T // tm, F // tf),
        in_specs=[pl.BlockSpec((tm, D), lambda i, f: (i, 0)), pl.BlockSpec((1, D), lambda i, f: (0, 0)),
                  pl.BlockSpec((D, tf), lambda i, f: (0, f)), pl.BlockSpec((tf, D), lambda i, f: (f, 0)),
                  pl.BlockSpec((1, D), lambda i, f: (0, 0))],
        out_specs=pl.BlockSpec((tm, D), lambda i, f: (i, 0)),
        out_shape=jax.ShapeDtypeStruct((T, D), F32),
        scratch_shapes=[pltpu.VMEM((tm, D), BF16)],
        compiler_params=_params("parallel", "arbitrary"),
        name="mlp",
    )(x2, g, w1, w2, gf)


def _odd_prep_kernel(x_ref, g_ref, w_ref, gq_ref, wuq_ref, gkv_ref, wukv_ref, cos_ref, sin_ref,
                     q_ref, k_ref, v_ref, xr_ref, xg_ref, *, q_scale):
    h = _rms(x_ref[...], g_ref[...]).astype(BF16)
    z = jnp.dot(h, w_ref[...], preferred_element_type=F32)
    cos = cos_ref[...]
    sin = sin_ref[...]
    o1 = C_Q_RANK
    o2 = o1 + C_KV_RANK
    o3 = o2 + LANES
    cq = _rms(z[:, :o1], gq_ref[...]).astype(BF16)
    ckv = _rms(z[:, o1:o2], gkv_ref[...]).astype(BF16)
    kr = _rope_slab(z[:, o2:o3], cos, sin, C_ROPE // 2)
    q = jnp.dot(cq, wuq_ref[...], preferred_element_type=F32)
    kv = jnp.dot(ckv, wukv_ref[...], preferred_element_type=F32)
    lane = lax.broadcasted_iota(jnp.int32, (1, LANES), 1)
    ones_col = (lane == C_VDIM).astype(F32)
    for hd in range(C_HEADS):
        qs = _rope_slab(q[:, hd * LANES:(hd + 1) * LANES], cos, sin, C_ROPE // 2)
        q_ref[hd] = (qs * q_scale).astype(BF16)
        k_ref[hd] = (kv[:, hd * LANES:(hd + 1) * LANES] + kr).astype(BF16)
        vo = (C_HEADS + hd) * LANES
        v_ref[hd] = (kv[:, vo:vo + LANES] + ones_col).astype(BF16)
    xr_ref[...] = z[:, o3:o3 + D_WIDTH]
    xg_ref[...] = z[:, o3 + D_WIDTH:]


def _odd_prep(x2, g, w, gq, wuq, gkv, wukv, cos, sin, seq, tm, q_scale):
    T, D = x2.shape
    nt_seq = seq // tm
    tok = lambda i: (i, 0)
    tab = lambda i: (i % nt_seq, 0)
    fixed = lambda i: (0, 0)
    head = lambda i: (0, i, 0)
    hshape = jax.ShapeDtypeStruct((C_HEADS, T, LANES), BF16)
    return pl.pallas_call(
        functools.partial(_odd_prep_kernel, q_scale=q_scale),
        grid=(T // tm,),
        in_specs=[pl.BlockSpec((tm, D), tok), pl.BlockSpec((1, D), fixed), pl.BlockSpec(w.shape, fixed),
                  pl.BlockSpec((1, C_Q_RANK), fixed), pl.BlockSpec(wuq.shape, fixed),
                  pl.BlockSpec((1, C_KV_RANK), fixed), pl.BlockSpec(wukv.shape, fixed),
                  pl.BlockSpec((tm, LANES), tab), pl.BlockSpec((tm, LANES), tab)],
        out_specs=[pl.BlockSpec((C_HEADS, tm, LANES), head), pl.BlockSpec((C_HEADS, tm, LANES), head),
                   pl.BlockSpec((C_HEADS, tm, LANES), head),
                   pl.BlockSpec((tm, D_WIDTH), tok), pl.BlockSpec((tm, D_WIDTH), tok)],
        out_shape=[hshape, hshape, hshape,
                   jax.ShapeDtypeStruct((T, D_WIDTH), F32), jax.ShapeDtypeStruct((T, D_WIDTH), F32)],
        compiler_params=_params("parallel"),
        name="odd_prep",
    )(x2, g, w, gq, wuq, gkv, wukv, cos, sin)


def _flash_kernel(q_ref, k_ref, vt_ref, o_ref, *, tk, n_chunks):
    q = q_ref[...]
    tq = q.shape[0]

    def body(c, carry):
        m, acc = carry
        start = pl.multiple_of(c * tk, tk)
        kc = k_ref[pl.ds(start, tk), :]
        st = lax.dot_general(kc, q, (((1,), (1,)), ((), ())), preferred_element_type=F32)
        m_new = jnp.maximum(m, jnp.max(st, axis=0, keepdims=True))
        p = jnp.exp2(st - m_new).astype(BF16)
        alpha = jnp.exp2(m - m_new)
        acc = alpha * acc + jnp.dot(vt_ref[c], p, preferred_element_type=F32)
        return m_new, acc

    m0 = jnp.full((1, tq), NEG_BIG, F32)
    acc0 = jnp.zeros((LANES, tq), F32)
    _, acc = lax.fori_loop(0, n_chunks, body, (m0, acc0))
    out = acc / acc[C_VDIM:C_VDIM + 1, :]
    o_ref[...] = out.T.astype(BF16)


def _flash(q, k, vt, batch, seq, tq, tk):
    H, T, _ = q.shape
    nq = seq // tq
    n_chunks = seq // tk
    return pl.pallas_call(
        functools.partial(_flash_kernel, tk=tk, n_chunks=n_chunks),
        grid=(batch, H, nq),
        in_specs=[pl.BlockSpec((None, tq, LANES), lambda b, h, i: (h, b * nq + i, 0)),
                  pl.BlockSpec((None, seq, LANES), lambda b, h, i: (h, b, 0)),
                  pl.BlockSpec((None, n_chunks, LANES, tk), lambda b, h, i: (h, b, 0, 0))],
        out_specs=pl.BlockSpec((None, tq, LANES), lambda b, h, i: (h, b * nq + i, 0)),
        out_shape=jax.ShapeDtypeStruct((H, T, LANES), BF16),
        compiler_params=_params("parallel", "parallel", "arbitrary"),
        name="mla_flash",
    )(q, k, vt)


def _scan_tile(a, b, carry, reverse):
    rows = a.shape[0]
    sub = lax.broadcasted_iota(jnp.int32, (rows, 1), 0) % SUBLANES
    k = 1
    while k < SUBLANES:
        shift = (rows - k) if reverse else k
        ok = (sub < SUBLANES - k) if reverse else (sub >= k)
        a_s = pltpu.roll(a, shift, 0)
        b_s = pltpu.roll(b, shift, 0)
        b = jnp.where(ok, a * b_s + b, b)
        a = jnp.where(ok, a * a_s, a)
        k *= 2
    groups = rows // SUBLANES
    order = range(groups - 1, -1, -1) if reverse else range(groups)
    edge = 0 if reverse else SUBLANES - 1
    out = [None] * groups
    for gi in order:
        sl = slice(gi * SUBLANES, (gi + 1) * SUBLANES)
        hg = a[sl] * carry + b[sl]
        out[gi] = hg
        carry = hg[edge:edge + 1]
    return jnp.concatenate(out, axis=0), carry


def _lru_gates(xp_ref, xc_ref, xn_ref, lo_edge, hi_edge, cw_ref, cb_ref, wg_ref, bg_ref, lam_row, ext_ref):
    tm = xc_ref.shape[0]
    ext_ref[0:SUBLANES, :] = jnp.where(lo_edge, 0.0, xp_ref[...])
    ext_ref[SUBLANES:SUBLANES + tm, :] = xc_ref[...]
    ext_ref[SUBLANES + tm:, :] = jnp.where(hi_edge, 0.0, xn_ref[...])
    left = CONV_WIDTH // 2
    xc = cb_ref[...] + cw_ref[0:1, :] * ext_ref[SUBLANES - left:SUBLANES - left + tm, :]
    for j in range(1, CONV_WIDTH):
        xc = xc + cw_ref[j:j + 1, :] * ext_ref[SUBLANES - left + j:SUBLANES - left + j + tm, :]
    gates = jnp.dot(xc.astype(BF16), wg_ref[...], preferred_element_type=F32) + bg_ref[...]
    gates = 1.0 / (1.0 + jnp.exp(-gates))
    r = gates[:, :D_WIDTH]
    ig = gates[:, D_WIDTH:]
    neg = -lam_row
    softplus = jnp.maximum(neg, 0.0) + jnp.log1p(jnp.exp(-jnp.abs(neg)))
    log_a = (-LRU_C) * r * softplus
    a = jnp.exp(log_a)
    th = jnp.tanh(log_a)
    b = jnp.sqrt(-2.0 * th / (1.0 - th)) * (ig * xc)
    return a, b


def _lru_kernel(fp_ref, fc_ref, fn_ref, bp_ref, bc_ref, bn_ref, cw_ref, cb_ref, wgf_ref, bgf_ref,
                wgb_ref, bgb_ref, lam_ref, hf_ref, hb_ref, ext_ref, carry_ref):
    i = pl.program_id(1)
    last = pl.num_programs(1) - 1

    @pl.when(i == 0)
    def _():
        carry_ref[...] = jnp.zeros_like(carry_ref)

    a, b = _lru_gates(fp_ref, fc_ref, fn_ref, i == 0, i == last, cw_ref, cb_ref, wgf_ref, bgf_ref,
                      lam_ref[0:1, :], ext_ref)
    h, cf = _scan_tile(a, b, carry_ref[0:1, :], reverse=False)
    hf_ref[...] = h
    carry_ref[0:1, :] = cf

    a, b = _lru_gates(bp_ref, bc_ref, bn_ref, i == last, i == 0, cw_ref, cb_ref, wgb_ref, bgb_ref,
                      lam_ref[1:2, :], ext_ref)
    h, cb = _scan_tile(a, b, carry_ref[1:2, :], reverse=True)
    hb_ref[...] = h
    carry_ref[1:2, :] = cb


def _lru(xr, cw, cb, wgf, bgf, wgb, bgb, lam, batch, seq, tm):
    T, W = xr.shape
    nt = seq // tm
    hb = tm // SUBLANES
    n_hb = T // SUBLANES
    f_tok = lambda b, i: (b * nt + i, 0)
    b_tok = lambda b, i: (b * nt + nt - 1 - i, 0)
    f_prev = lambda b, i: (jnp.maximum((b * nt + i) * hb - 1, 0), 0)
    f_next = lambda b, i: (jnp.minimum((b * nt + i + 1) * hb, n_hb - 1), 0)
    b_prev = lambda b, i: (jnp.maximum((b * nt + nt - 1 - i) * hb - 1, 0), 0)
    b_next = lambda b, i: (jnp.minimum((b * nt + nt - i) * hb, n_hb - 1), 0)
    fixed = lambda b, i: (0, 0)
    halo = (SUBLANES, W)
    return pl.pallas_call(
        _lru_kernel,
        grid=(batch, nt),
        in_specs=[pl.BlockSpec(halo, f_prev), pl.BlockSpec((tm, W), f_tok), pl.BlockSpec(halo, f_next),
                  pl.BlockSpec(halo, b_prev), pl.BlockSpec((tm, W), b_tok), pl.BlockSpec(halo, b_next),
                  pl.BlockSpec(cw.shape, fixed), pl.BlockSpec(cb.shape, fixed),
                  pl.BlockSpec(wgf.shape, fixed), pl.BlockSpec(bgf.shape, fixed),
                  pl.BlockSpec(wgb.shape, fixed), pl.BlockSpec(bgb.shape, fixed),
                  pl.BlockSpec(lam.shape, fixed)],
        out_specs=[pl.BlockSpec((tm, W), f_tok), pl.BlockSpec((tm, W), b_tok)],
        out_shape=[jax.ShapeDtypeStruct((T, W), F32), jax.ShapeDtypeStruct((T, W), F32)],
        scratch_shapes=[pltpu.VMEM((tm + 2 * SUBLANES, W), F32), pltpu.VMEM((2, W), F32)],
        compiler_params=_params("arbitrary", "arbitrary"),
        name="rglru",
    )(xr, xr, xr, xr, xr, xr, cw, cb, wgf, bgf, wgb, bgb, lam)


def _gelu_tanh(x):
    return 0.5 * x * (1.0 + jnp.tanh(math.sqrt(2.0 / math.pi) * (x + 0.044715 * (x * x * x))))


def _odd_out_kernel(x_ref, yc_ref, hf_ref, hb_ref, xg_ref, wc_ref, wd_ref, o_ref):
    yd = ((hf_ref[...] + hb_ref[...]) * _gelu_tanh(xg_ref[...])).astype(BF16)
    acc = x_ref[...] + jnp.dot(yd, wd_ref[...], preferred_element_type=F32)
    for hd in range(C_HEADS):
        acc = acc + jnp.dot(yc_ref[hd], wc_ref[hd], preferred_element_type=F32)
    o_ref[...] = acc


def _odd_out(x2, yc, hf, hb, xg, wc, wd, tm):
    T, D = x2.shape
    tok = lambda i: (i, 0)
    return pl.pallas_call(
        _odd_out_kernel,
        grid=(T // tm,),
        in_specs=[pl.BlockSpec((tm, D), tok), pl.BlockSpec((C_HEADS, tm, LANES), lambda i: (0, i, 0)),
                  pl.BlockSpec((tm, D_WIDTH), tok), pl.BlockSpec((tm, D_WIDTH), tok), pl.BlockSpec((tm, D_WIDTH), tok),
                  pl.BlockSpec(wc.shape, lambda i: (0, 0, 0)), pl.BlockSpec(wd.shape, lambda i: (0, 0))],
        out_specs=pl.BlockSpec((tm, D), tok),
        out_shape=jax.ShapeDtypeStruct((T, D), F32),
        compiler_params=_params("parallel"),
        name="odd_out",
    )(x2, yc, hf, hb, xg, wc, wd)


def _rope_tables(seq, half, lane_lo, lane_hi):
    inv = ROPE_THETA ** (-jnp.arange(half, dtype=F32) / half)
    ang = jnp.arange(seq, dtype=F32)[:, None] * inv[None, :]
    lane = jnp.arange(LANES)
    idx = lane % half
    sign = jnp.where((lane % (2 * half)) < half, -1.0, 1.0).astype(F32)
    active = (lane >= lane_lo) & (lane < lane_hi)
    cos = jnp.where(active[None, :], jnp.cos(ang)[:, idx], 1.0)
    sin = jnp.where(active[None, :], jnp.sin(ang)[:, idx] * sign[None, :], 0.0)
    return cos, sin


def _pad_cols(w, width):
    return jnp.pad(w, ((0, 0), (0, width - w.shape[1])))


def _block_diag(w):
    n, d, _ = w.shape
    eye = jnp.eye(n, dtype=w.dtype)
    return (eye[:, None, :, None] * w[:, :, None, :]).reshape(n * d, n * d)


def _tile_of(n, pref):
    t = min(n, pref)
    assert n % t == 0
    return t


def kernel(x, e_norm_mix, e_w_in, e_sink, e_w_pool, e_pool_scale, e_w_out, o_norm_mix, o_w_in, o_g_cq, o_w_uq,
           o_g_ckv, o_w_ukv, o_conv_w, o_conv_b, o_lru_wa, o_lru_ba, o_lru_wx, o_lru_bx, o_lru_lambda, o_w_out,
           norm_mlp, w_mlp1, w_mlp2, final_norm):
    batch, seq, D = x.shape
    T = batch * seq
    depth = norm_mlp.shape[0]
    x2 = x.reshape(T, D)

    tm_prep = _tile_of(seq, 512)
    tq_even = _tile_of(seq, 256)
    tm_mlp = _tile_of(T, 1024)
    tf_mlp = _tile_of(w_mlp1.shape[2], 512)
    tq_flash = _tile_of(seq, 256)
    tk_flash = _tile_of(seq, 512)
    tm_lru = _tile_of(seq, 256)
    tm_out = _tile_of(seq, 512)

    cos_a, sin_a = _rope_tables(seq, A_HEAD_DIM // 2, 0, LANES)
    cos_c, sin_c = _rope_tables(seq, C_ROPE // 2, C_NOPE, C_NOPE + C_ROPE)
    q_scale_c = (C_NOPE + C_ROPE) ** -0.5 * LOG2E
    gf = final_norm.reshape(1, D)

    for layer in range(depth):
        li = layer // 2
        if layer % 2 == 0:
            w = e_w_in[li]
            kq = A_WIDTH
            k0 = w[:, kq:kq + A_HEAD_DIM]
            k1 = w[:, kq + A_HEAD_DIM:kq + 2 * A_HEAD_DIM]
            vq = kq + 2 * A_HEAD_DIM
            v0 = w[:, vq:vq + A_HEAD_DIM]
            v1 = w[:, vq + A_HEAD_DIM:vq + 2 * A_HEAD_DIM]
            w_in = jnp.concatenate([w[:, :kq], k0, k0, k1, k1, v0, v0, v1, v1, w[:, vq + 2 * A_HEAD_DIM:]],
                                   axis=1).astype(BF16)
            q, k, v, u = _even_prep(x2, e_norm_mix[li].reshape(1, D), w_in, cos_a, sin_a, seq, tm_prep)
            x2 = _even_mix(x2, q, k, v, u, e_sink[li], e_w_pool[li].astype(BF16), e_pool_scale[li].reshape(1, -1),
                           e_w_out[li].astype(BF16), seq, tq_even)
        else:
            w = o_w_in[li]
            i1 = C_Q_RANK
            i2 = i1 + C_KV_RANK
            i3 = i2 + C_ROPE
            kr_slab = jnp.pad(w[:, i2:i3], ((0, 0), (C_NOPE, LANES - C_NOPE - C_ROPE)))
            w_in = jnp.concatenate([w[:, :i2], kr_slab, w[:, i3:]], axis=1).astype(BF16)
            wuq = o_w_uq[li].reshape(C_Q_RANK, C_HEADS, C_NOPE + C_ROPE)
            wuq = jnp.pad(wuq, ((0, 0), (0, 0), (0, LANES - C_NOPE - C_ROPE))).reshape(C_Q_RANK, C_HEADS * LANES)
            wukv = o_w_ukv[li].reshape(C_KV_RANK, C_HEADS, C_NOPE + C_VDIM)
            wk = jnp.pad(wukv[:, :, :C_NOPE], ((0, 0), (0, 0), (0, LANES - C_NOPE)))
            wv = jnp.pad(wukv[:, :, C_NOPE:], ((0, 0), (0, 0), (0, LANES - C_VDIM)))
            wukv = jnp.concatenate([wk.reshape(C_KV_RANK, -1), wv.reshape(C_KV_RANK, -1)], axis=1)
            q, k, v, xr, xg = _odd_prep(x2, o_norm_mix[li].reshape(1, D), w_in, o_g_cq[li].reshape(1, -1),
                                        wuq.astype(BF16), o_g_ckv[li].reshape(1, -1), wukv.astype(BF16),
                                        cos_c, sin_c, seq, tm_prep, q_scale_c)
            n_chunks = seq // tk_flash
            vt = v.reshape(C_HEADS, batch, n_chunks, tk_flash, LANES).swapaxes(3, 4)
            vt = vt.reshape(C_HEADS, batch * n_chunks, LANES, tk_flash)
            yc = _flash(q, k, vt, batch, seq, tq_flash, tk_flash)

            def gate_w(d):
                return jnp.concatenate([_block_diag(o_lru_wa[li, d]), _block_diag(o_lru_wx[li, d])], axis=1).astype(BF16)

            def gate_b(d):
                return jnp.concatenate([o_lru_ba[li, d], o_lru_bx[li, d]]).reshape(1, -1)

            hf, hb = _lru(xr, o_conv_w[li], o_conv_b[li].reshape(1, -1), gate_w(0), gate_b(0), gate_w(1), gate_b(1),
                          o_lru_lambda[li], batch, seq, tm_lru)
            wo = o_w_out[li]
            wc = wo[:C_HEADS * C_VDIM].reshape(C_HEADS, C_VDIM, D)
            wc = jnp.pad(wc, ((0, 0), (0, LANES - C_VDIM), (0, 0))).astype(BF16)
            x2 = _odd_out(x2, yc, hf, hb, xg, wc, wo[C_HEADS * C_VDIM:].astype(BF16), tm_out)
        x2 = _mlp(x2, norm_mlp[layer].reshape(1, D), w_mlp1[layer].astype(BF16), w_mlp2[layer].astype(BF16), gf,
                  tm_mlp, tf_mlp, final_norm=(layer == depth - 1))
    return x2.reshape(batch, seq, D)
```

```python
import functools
import math

import jax
import jax.numpy as jnp
from jax import lax
from jax.experimental import pallas as pl
from jax.experimental.pallas import tpu as pltpu

F32 = jnp.float32
BF16 = jnp.bfloat16

EPS = 1e-6
ROPE_THETA = 10000.0
LANES = 128
SUBLANES = 8
VMEM_LIMIT = 48 * 1024 * 1024

A_HEADS = 8
A_KV_HEADS = 2
A_HEAD_DIM = 64
A_WINDOW = 128
A_WIDTH = A_HEADS * A_HEAD_DIM
POOL_WINDOWS = (2, 4, 8, 16)
B_GROUP_DIM = 128
B_WIDTH = B_GROUP_DIM * len(POOL_WINDOWS)
POOL_HALO = 8
C_HEADS = 8
C_NOPE = 64
C_ROPE = 32
C_VDIM = 64
C_Q_RANK = 256
C_KV_RANK = 128
V_ROWS = 80
D_WIDTH = 512
D_BLOCKS = 8
D_BLOCK_DIM = 64
CONV_WIDTH = 4
LRU_C = 8.0

NEG_BIG = -1e30
LOG2E = math.log2(math.e)


def _params(*sem):
    return pltpu.CompilerParams(dimension_semantics=sem, vmem_limit_bytes=VMEM_LIMIT)


def _rms(x, g):
    return x * lax.rsqrt(jnp.mean(x * x, axis=-1, keepdims=True) + EPS) * g


def _rope_slab(xs, cos, sin, half):
    lane = lax.broadcasted_iota(jnp.int32, (1, LANES), 1)
    first = (lane % (2 * half)) < half
    swapped = jnp.where(first, pltpu.roll(xs, LANES - half, 1), pltpu.roll(xs, half, 1))
    return xs * cos + swapped * sin


def _even_prep_kernel(x_ref, g_ref, w_ref, cos_ref, sin_ref, q_ref, k_ref, v_ref, u_ref):
    h = _rms(x_ref[...], g_ref[...]).astype(BF16)
    z = jnp.dot(h, w_ref[...], preferred_element_type=F32)
    cos = cos_ref[...]
    sin = sin_ref[...]
    n_q = A_WIDTH // LANES
    for c in range(n_q):
        r = _rope_slab(z[:, c * LANES:(c + 1) * LANES], cos, sin, A_HEAD_DIM // 2)
        q_ref[:, c * LANES:(c + 1) * LANES] = (r * (A_HEAD_DIM ** -0.5)).astype(BF16)
    for c in range(A_KV_HEADS):
        lo = A_WIDTH + c * LANES
        k_ref[:, c * LANES:(c + 1) * LANES] = _rope_slab(z[:, lo:lo + LANES], cos, sin, A_HEAD_DIM // 2).astype(BF16)
    lo = A_WIDTH + A_KV_HEADS * LANES
    v_ref[...] = z[:, lo:lo + A_KV_HEADS * LANES].astype(BF16)
    u_ref[...] = z[:, lo + A_KV_HEADS * LANES:]


def _even_prep(x2, g, w, cos, sin, seq, tm):
    T, D = x2.shape
    n_in = w.shape[1]
    kvw = A_KV_HEADS * LANES
    nt_seq = seq // tm
    tok = lambda i: (i, 0)
    tab = lambda i: (i % nt_seq, 0)
    fixed = lambda i: (0, 0)
    return pl.pallas_call(
        _even_prep_kernel,
        grid=(T // tm,),
        in_specs=[pl.BlockSpec((tm, D), tok), pl.BlockSpec((1, D), fixed), pl.BlockSpec((D, n_in), fixed),
                  pl.BlockSpec((tm, LANES), tab), pl.BlockSpec((tm, LANES), tab)],
        out_specs=[pl.BlockSpec((tm, A_WIDTH), tok), pl.BlockSpec((tm, kvw), tok),
                   pl.BlockSpec((tm, kvw), tok), pl.BlockSpec((tm, B_WIDTH), tok)],
        out_shape=[jax.ShapeDtypeStruct((T, A_WIDTH), BF16), jax.ShapeDtypeStruct((T, kvw), BF16),
                   jax.ShapeDtypeStruct((T, kvw), BF16), jax.ShapeDtypeStruct((T, B_WIDTH), F32)],
        compiler_params=_params("parallel"),
        name="even_prep",
    )(x2, g, w, cos, sin)


def _even_mix_kernel(sink_ref, x_ref, q_ref, kp_ref, kc_ref, kn_ref, vp_ref, vc_ref, vn_ref,
                     up_ref, uc_ref, un_ref, wpool_ref, pscale_ref, wout_ref, o_ref, uext_ref, *, seq, tq):
    i = pl.program_id(0)
    nt_seq = seq // tq
    pos0 = (i % nt_seq) * tq
    at_start = pos0 == 0
    at_end = pos0 + tq == seq
    nk = tq + 2 * A_WINDOW

    kd = jnp.concatenate([kp_ref[...], kc_ref[...], kn_ref[...]], axis=0)
    vd = jnp.concatenate([vp_ref[...], vc_ref[...], vn_ref[...]], axis=0)
    r = lax.broadcasted_iota(jnp.int32, (tq, nk), 0)
    c = lax.broadcasted_iota(jnp.int32, (tq, nk), 1)
    kpos = pos0 - A_WINDOW + c
    valid = (jnp.abs(c - A_WINDOW - r) <= A_WINDOW) & (kpos >= 0) & (kpos < seq)
    valid2 = jnp.concatenate([valid, valid], axis=0)
    lane = lax.broadcasted_iota(jnp.int32, (1, LANES), 1)
    low = lane < A_HEAD_DIM
    row2 = lax.broadcasted_iota(jnp.int32, (2 * tq, 1), 0)
    ya = []
    for j in range(A_HEADS // 2):
        g = (2 * j) // (A_HEADS // A_KV_HEADS)
        qp = q_ref[:, j * LANES:(j + 1) * LANES]
        zero = jnp.zeros_like(qp)
        lhs = jnp.concatenate([jnp.where(low, qp, zero), jnp.where(low, zero, qp)], axis=0)
        s = lax.dot_general(lhs, kd[:, g * LANES:(g + 1) * LANES], (((1,), (1,)), ((), ())),
                            preferred_element_type=F32)
        s = jnp.where(valid2, s, NEG_BIG)
        sk = jnp.where(row2 < tq, sink_ref[2 * j], sink_ref[2 * j + 1])
        m = jnp.maximum(jnp.max(s, axis=1, keepdims=True), sk)
        p = jnp.exp(s - m)
        den = jnp.sum(p, axis=1, keepdims=True) + jnp.exp(sk - m)
        o = jnp.dot(p.astype(BF16), vd[:, g * LANES:(g + 1) * LANES], preferred_element_type=F32) / den
        ya.append(jnp.where(low, o[:tq], o[tq:]))

    uext_ref[0:POOL_HALO, :] = jnp.where(at_start, 0.0, up_ref[...])
    uext_ref[POOL_HALO:POOL_HALO + tq, :] = uc_ref[...]
    uext_ref[POOL_HALO + tq:, :] = jnp.where(at_end, 0.0, un_ref[...])
    t = pos0 + lax.broadcasted_iota(jnp.int32, (tq, 1), 0)
    yb = []
    for gi, w in enumerate(POOL_WINDOWS):
        half = w // 2
        cols = slice(gi * B_GROUP_DIM, (gi + 1) * B_GROUP_DIM)
        win = uext_ref[POOL_HALO - half:POOL_HALO - half + tq, cols]
        for off in range(-half + 1, half):
            win = win + uext_ref[POOL_HALO + off:POOL_HALO + off + tq, cols]
        cnt = (jnp.minimum(t + half, seq) - jnp.maximum(t - half, 0)).astype(F32)
        d = win / cnt - uc_ref[:, cols]
        y = jnp.dot(d.astype(BF16), wpool_ref[gi], preferred_element_type=F32)
        yb.append(y * pscale_ref[:, cols])

    ycat = jnp.concatenate(ya + yb, axis=1).astype(BF16)
    o_ref[...] = x_ref[...] + jnp.dot(ycat, wout_ref[...], preferred_element_type=F32)


def _even_mix(x2, q, k, v, u, sink, wpool, pscale, wout, seq, tq):
    T, D = x2.shape
    kvw = k.shape[1]
    nt_seq = seq // tq
    wb = tq // A_WINDOW
    hb = tq // POOL_HALO
    n_wb = T // A_WINDOW
    n_hb = T // POOL_HALO
    tok = lambda i: (i, 0)
    prev_w = lambda i: (jnp.maximum(i * wb - 1, 0), 0)
    next_w = lambda i: (jnp.minimum((i + 1) * wb, n_wb - 1), 0)
    prev_h = lambda i: (jnp.maximum(i * hb - 1, 0), 0)
    next_h = lambda i: (jnp.minimum((i + 1) * hb, n_hb - 1), 0)
    fixed2 = lambda i: (0, 0)
    fixed3 = lambda i: (0, 0, 0)
    return pl.pallas_call(
        functools.partial(_even_mix_kernel, seq=seq, tq=tq),
        grid=(T // tq,),
        in_specs=[pl.BlockSpec(memory_space=pltpu.SMEM),
                  pl.BlockSpec((tq, D), tok),
                  pl.BlockSpec((tq, A_WIDTH), tok),
                  pl.BlockSpec((A_WINDOW, kvw), prev_w), pl.BlockSpec((tq, kvw), tok), pl.BlockSpec((A_WINDOW, kvw), next_w),
                  pl.BlockSpec((A_WINDOW, kvw), prev_w), pl.BlockSpec((tq, kvw), tok), pl.BlockSpec((A_WINDOW, kvw), next_w),
                  pl.BlockSpec((POOL_HALO, B_WIDTH), prev_h), pl.BlockSpec((tq, B_WIDTH), tok), pl.BlockSpec((POOL_HALO, B_WIDTH), next_h),
                  pl.BlockSpec(wpool.shape, fixed3), pl.BlockSpec((1, B_WIDTH), fixed2), pl.BlockSpec((D, D), fixed2)],
        out_specs=pl.BlockSpec((tq, D), tok),
        out_shape=jax.ShapeDtypeStruct((T, D), F32),
        scratch_shapes=[pltpu.VMEM((tq + 2 * POOL_HALO, B_WIDTH), F32)],
        compiler_params=_params("parallel"),
        name="even_mix",
    )(sink, x2, q, k, k, k, v, v, v, u, u, u, wpool, pscale, wout)


def _mlp_kernel(x_ref, g_ref, w1_ref, w2_ref, gf_ref, o_ref, h_ref, *, final_norm):
    f = pl.program_id(1)

    @pl.when(f == 0)
    def _():
        x = x_ref[...]
        h_ref[...] = _rms(x, g_ref[...]).astype(BF16)
        o_ref[...] = x

    u = jnp.maximum(jnp.dot(h_ref[...], w1_ref[...], preferred_element_type=F32), 0.0)
    o_ref[...] += jnp.dot((u * u).astype(BF16), w2_ref[...], preferred_element_type=F32)

    if final_norm:
        @pl.when(f == pl.num_programs(1) - 1)
        def _():
            o_ref[...] = _rms(o_ref[...], gf_ref[...])


def _mlp(x2, g, w1, w2, gf, tm, tf, final_norm):
    T, D = x2.shape
    F = w1.shape[1]
    return pl.pallas_call(
        functools.partial(_mlp_kernel, final_norm=final_norm),
        grid=(T // tm, F // tf),
        in_specs=[pl.BlockSpec((tm, D), lambda i, f: (i, 0)), pl.BlockSpec((1, D), lambda i, f: (0, 0)),
                  pl.BlockSpec((D, tf), lambda i, f: (0, f)), pl.BlockSpec((tf, D), lambda i, f: (f, 0)),
                  pl.BlockSpec((1, D), lambda i, f: (0, 0))],
        out_specs=pl.BlockSpec((tm, D), lambda i, f: (i, 0)),
        out_shape=jax.ShapeDtypeStruct((T, D), F32),
        scratch_shapes=[pltpu.VMEM((tm, D), BF16)],
        compiler_params=_params("parallel", "arbitrary"),
        name="mlp",
    )(x2, g, w1, w2, gf)


def _odd_prep_kernel(x_ref, g_ref, w_ref, gq_ref, wuq_ref, gkv_ref, wukv_ref, cos_ref, sin_ref,
                     q_ref, k_ref, v_ref, xr_ref, xg_ref, *, q_scale):
    h = _rms(x_ref[...], g_ref[...]).astype(BF16)
    z = jnp.dot(h, w_ref[...], preferred_element_type=F32)
    cos = cos_ref[...]
    sin = sin_ref[...]
    o1 = C_Q_RANK
    o2 = o1 + C_KV_RANK
    o3 = o2 + LANES
    cq = _rms(z[:, :o1], gq_ref[...]).astype(BF16)
    ckv = _rms(z[:, o1:o2], gkv_ref[...]).astype(BF16)
    kr = _rope_slab(z[:, o2:o3], cos, sin, C_ROPE // 2)
    q = jnp.dot(cq, wuq_ref[...], preferred_element_type=F32)
    kv = jnp.dot(ckv, wukv_ref[...], preferred_element_type=F32)
    lane = lax.broadcasted_iota(jnp.int32, (1, LANES), 1)
    ones_col = (lane == C_VDIM).astype(F32)
    for hd in range(C_HEADS):
        qs = _rope_slab(q[:, hd * LANES:(hd + 1) * LANES], cos, sin, C_ROPE // 2)
        q_ref[hd] = (qs * q_scale).astype(BF16)
        k_ref[hd] = (kv[:, hd * LANES:(hd + 1) * LANES] + kr).astype(BF16)
        vo = (C_HEADS + hd) * LANES
        v_ref[hd] = (kv[:, vo:vo + LANES] + ones_col).astype(BF16)
    xr_ref[...] = z[:, o3:o3 + D_WIDTH]
    xg_ref[...] = z[:, o3 + D_WIDTH:]


def _odd_prep(x2, g, w, gq, wuq, gkv, wukv, cos, sin, seq, tm, q_scale):
    T, D = x2.shape
    nt_seq = seq // tm
    tok = lambda i: (i, 0)
    tab = lambda i: (i % nt_seq, 0)
    fixed = lambda i: (0, 0)
    head = lambda i: (0, i, 0)
    hshape = jax.ShapeDtypeStruct((C_HEADS, T, LANES), BF16)
    return pl.pallas_call(
        functools.partial(_odd_prep_kernel, q_scale=q_scale),
        grid=(T // tm,),
        in_specs=[pl.BlockSpec((tm, D), tok), pl.BlockSpec((1, D), fixed), pl.BlockSpec(w.shape, fixed),
                  pl.BlockSpec((1, C_Q_RANK), fixed), pl.BlockSpec(wuq.shape, fixed),
                  pl.BlockSpec((1, C_KV_RANK), fixed), pl.BlockSpec(wukv.shape, fixed),
                  pl.BlockSpec((tm, LANES), tab), pl.BlockSpec((tm, LANES), tab)],
        out_specs=[pl.BlockSpec((C_HEADS, tm, LANES), head), pl.BlockSpec((C_HEADS, tm, LANES), head),
                   pl.BlockSpec((C_HEADS, tm, LANES), head),
                   pl.BlockSpec((tm, D_WIDTH), tok), pl.BlockSpec((tm, D_WIDTH), tok)],
        out_shape=[hshape, hshape, hshape,
                   jax.ShapeDtypeStruct((T, D_WIDTH), F32), jax.ShapeDtypeStruct((T, D_WIDTH), F32)],
        compiler_params=_params("parallel"),
        name="odd_prep",
    )(x2, g, w, gq, wuq, gkv, wukv, cos, sin)


def _flash_kernel(q_ref, k_ref, vt_ref, o_ref, s_ref, p_ref, acc_ref, *, tk, n_chunks):
    q = q_ref[...]
    tq = q.shape[0]

    def scores(c):
        start = pl.multiple_of(c * tk, tk)
        st = lax.dot_general(k_ref[pl.ds(start, tk), :], q, (((1,), (1,)), ((), ())),
                             preferred_element_type=F32)
        s_ref[...] = st
        return jnp.max(st, axis=0, keepdims=True)

    def values(c, alpha):
        acc_ref[...] = alpha * acc_ref[...] + jnp.dot(vt_ref[c], p_ref[...], preferred_element_type=F32)

    p_ref[...] = jnp.zeros_like(p_ref)
    acc_ref[...] = jnp.zeros_like(acc_ref)
    mx0 = scores(0)

    def body(c, carry):
        m, mx, alpha = carry
        values(jnp.maximum(c - 1, 0), alpha)
        m_new = jnp.maximum(m, mx)
        p_ref[...] = jnp.exp2(s_ref[...] - m_new).astype(BF16)
        mx_next = scores(jnp.minimum(c + 1, n_chunks - 1))
        return m_new, mx_next, jnp.exp2(m - m_new)

    m0 = jnp.full((1, tq), NEG_BIG, F32)
    _, _, alpha = lax.fori_loop(0, n_chunks, body, (m0, mx0, jnp.ones((1, tq), F32)))
    values(n_chunks - 1, alpha)
    acc = acc_ref[...]
    out = acc[:C_VDIM] / acc[C_VDIM:C_VDIM + 1, :]
    out = jnp.concatenate([out, jnp.zeros((LANES - C_VDIM, tq), F32)], axis=0)
    o_ref[...] = out.T.astype(BF16)


def _flash(q, k, vt, batch, seq, tq, tk):
    H, T, _ = q.shape
    nq = seq // tq
    n_chunks = seq // tk
    return pl.pallas_call(
        functools.partial(_flash_kernel, tk=tk, n_chunks=n_chunks),
        grid=(batch, H, nq),
        in_specs=[pl.BlockSpec((None, tq, LANES), lambda b, h, i: (h, b * nq + i, 0)),
                  pl.BlockSpec((None, seq, LANES), lambda b, h, i: (h, b, 0)),
                  pl.BlockSpec((None, n_chunks, V_ROWS, tk), lambda b, h, i: (h, b, 0, 0))],
        out_specs=pl.BlockSpec((None, tq, LANES), lambda b, h, i: (h, b * nq + i, 0)),
        out_shape=jax.ShapeDtypeStruct((H, T, LANES), BF16),
        scratch_shapes=[pltpu.VMEM((tk, tq), F32), pltpu.VMEM((tk, tq), BF16), pltpu.VMEM((V_ROWS, tq), F32)],
        compiler_params=_params("parallel", "parallel", "arbitrary"),
        name="mla_flash",
    )(q, k, vt)


def _scan_tile(a, b, carry, reverse):
    rows = a.shape[0]
    sub = lax.broadcasted_iota(jnp.int32, (rows, 1), 0) % SUBLANES
    k = 1
    while k < SUBLANES:
        shift = (rows - k) if reverse else k
        ok = (sub < SUBLANES - k) if reverse else (sub >= k)
        a_s = pltpu.roll(a, shift, 0)
        b_s = pltpu.roll(b, shift, 0)
        b = jnp.where(ok, a * b_s + b, b)
        a = jnp.where(ok, a * a_s, a)
        k *= 2
    groups = rows // SUBLANES
    order = range(groups - 1, -1, -1) if reverse else range(groups)
    edge = 0 if reverse else SUBLANES - 1
    out = [None] * groups
    for gi in order:
        sl = slice(gi * SUBLANES, (gi + 1) * SUBLANES)
        hg = a[sl] * carry + b[sl]
        out[gi] = hg
        carry = hg[edge:edge + 1]
    return jnp.concatenate(out, axis=0), carry


def _lru_gates(xp_ref, xc_ref, xn_ref, lo_edge, hi_edge, cw_ref, cb_ref, wg_ref, bg_ref, lam_row, ext_ref):
    tm = xc_ref.shape[0]
    ext_ref[0:SUBLANES, :] = jnp.where(lo_edge, 0.0, xp_ref[...])
    ext_ref[SUBLANES:SUBLANES + tm, :] = xc_ref[...]
    ext_ref[SUBLANES + tm:, :] = jnp.where(hi_edge, 0.0, xn_ref[...])
    left = CONV_WIDTH // 2
    xc = cb_ref[...] + cw_ref[0:1, :] * ext_ref[SUBLANES - left:SUBLANES - left + tm, :]
    for j in range(1, CONV_WIDTH):
        xc = xc + cw_ref[j:j + 1, :] * ext_ref[SUBLANES - left + j:SUBLANES - left + j + tm, :]
    gates = jnp.dot(xc.astype(BF16), wg_ref[...], preferred_element_type=F32) + bg_ref[...]
    gates = 1.0 / (1.0 + jnp.exp(-gates))
    r = gates[:, :D_WIDTH]
    ig = gates[:, D_WIDTH:]
    neg = -lam_row
    softplus = jnp.maximum(neg, 0.0) + jnp.log1p(jnp.exp(-jnp.abs(neg)))
    log_a = (-LRU_C) * r * softplus
    a = jnp.exp(log_a)
    th = jnp.tanh(log_a)
    b = jnp.sqrt(-2.0 * th / (1.0 - th)) * (ig * xc)
    return a, b


def _lru_kernel(fp_ref, fc_ref, fn_ref, bp_ref, bc_ref, bn_ref, cw_ref, cb_ref, wgf_ref, bgf_ref,
                wgb_ref, bgb_ref, lam_ref, hf_ref, hb_ref, ext_ref, carry_ref):
    i = pl.program_id(1)
    last = pl.num_programs(1) - 1

    @pl.when(i == 0)
    def _():
        carry_ref[...] = jnp.zeros_like(carry_ref)

    a, b = _lru_gates(fp_ref, fc_ref, fn_ref, i == 0, i == last, cw_ref, cb_ref, wgf_ref, bgf_ref,
                      lam_ref[0:1, :], ext_ref)
    h, cf = _scan_tile(a, b, carry_ref[0:1, :], reverse=False)
    hf_ref[...] = h
    carry_ref[0:1, :] = cf

    a, b = _lru_gates(bp_ref, bc_ref, bn_ref, i == last, i == 0, cw_ref, cb_ref, wgb_ref, bgb_ref,
                      lam_ref[1:2, :], ext_ref)
    h, cb = _scan_tile(a, b, carry_ref[1:2, :], reverse=True)
    hb_ref[...] = h
    carry_ref[1:2, :] = cb


def _lru(xr, cw, cb, wgf, bgf, wgb, bgb, lam, batch, seq, tm):
    T, W = xr.shape
    nt = seq // tm
    hb = tm // SUBLANES
    n_hb = T // SUBLANES
    f_tok = lambda b, i: (b * nt + i, 0)
    b_tok = lambda b, i: (b * nt + nt - 1 - i, 0)
    f_prev = lambda b, i: (jnp.maximum((b * nt + i) * hb - 1, 0), 0)
    f_next = lambda b, i: (jnp.minimum((b * nt + i + 1) * hb, n_hb - 1), 0)
    b_prev = lambda b, i: (jnp.maximum((b * nt + nt - 1 - i) * hb - 1, 0), 0)
    b_next = lambda b, i: (jnp.minimum((b * nt + nt - i) * hb, n_hb - 1), 0)
    fixed = lambda b, i: (0, 0)
    halo = (SUBLANES, W)
    return pl.pallas_call(
        _lru_kernel,
        grid=(batch, nt),
        in_specs=[pl.BlockSpec(halo, f_prev), pl.BlockSpec((tm, W), f_tok), pl.BlockSpec(halo, f_next),
                  pl.BlockSpec(halo, b_prev), pl.BlockSpec((tm, W), b_tok), pl.BlockSpec(halo, b_next),
                  pl.BlockSpec(cw.shape, fixed), pl.BlockSpec(cb.shape, fixed),
                  pl.BlockSpec(wgf.shape, fixed), pl.BlockSpec(bgf.shape, fixed),
                  pl.BlockSpec(wgb.shape, fixed), pl.BlockSpec(bgb.shape, fixed),
                  pl.BlockSpec(lam.shape, fixed)],
        out_specs=[pl.BlockSpec((tm, W), f_tok), pl.BlockSpec((tm, W), b_tok)],
        out_shape=[jax.ShapeDtypeStruct((T, W), F32), jax.ShapeDtypeStruct((T, W), F32)],
        scratch_shapes=[pltpu.VMEM((tm + 2 * SUBLANES, W), F32), pltpu.VMEM((2, W), F32)],
        compiler_params=_params("arbitrary", "arbitrary"),
        name="rglru",
    )(xr, xr, xr, xr, xr, xr, cw, cb, wgf, bgf, wgb, bgb, lam)


def _gelu_tanh(x):
    return 0.5 * x * (1.0 + jnp.tanh(math.sqrt(2.0 / math.pi) * (x + 0.044715 * (x * x * x))))


def _odd_out_kernel(x_ref, yc_ref, hf_ref, hb_ref, xg_ref, wc_ref, wd_ref, o_ref):
    yd = ((hf_ref[...] + hb_ref[...]) * _gelu_tanh(xg_ref[...])).astype(BF16)
    acc = x_ref[...] + jnp.dot(yd, wd_ref[...], preferred_element_type=F32)
    for hd in range(C_HEADS):
        acc = acc + jnp.dot(yc_ref[hd], wc_ref[hd], preferred_element_type=F32)
    o_ref[...] = acc


def _odd_out(x2, yc, hf, hb, xg, wc, wd, tm):
    T, D = x2.shape
    tok = lambda i: (i, 0)
    return pl.pallas_call(
        _odd_out_kernel,
        grid=(T // tm,),
        in_specs=[pl.BlockSpec((tm, D), tok), pl.BlockSpec((C_HEADS, tm, LANES), lambda i: (0, i, 0)),
                  pl.BlockSpec((tm, D_WIDTH), tok), pl.BlockSpec((tm, D_WIDTH), tok), pl.BlockSpec((tm, D_WIDTH), tok),
                  pl.BlockSpec(wc.shape, lambda i: (0, 0, 0)), pl.BlockSpec(wd.shape, lambda i: (0, 0))],
        out_specs=pl.BlockSpec((tm, D), tok),
        out_shape=jax.ShapeDtypeStruct((T, D), F32),
        compiler_params=_params("parallel"),
        name="odd_out",
    )(x2, yc, hf, hb, xg, wc, wd)


def _rope_tables(seq, half, lane_lo, lane_hi):
    inv = ROPE_THETA ** (-jnp.arange(half, dtype=F32) / half)
    ang = jnp.arange(seq, dtype=F32)[:, None] * inv[None, :]
    lane = jnp.arange(LANES)
    idx = lane % half
    sign = jnp.where((lane % (2 * half)) < half, -1.0, 1.0).astype(F32)
    active = (lane >= lane_lo) & (lane < lane_hi)
    cos = jnp.where(active[None, :], jnp.cos(ang)[:, idx], 1.0)
    sin = jnp.where(active[None, :], jnp.sin(ang)[:, idx] * sign[None, :], 0.0)
    return cos, sin


def _pad_cols(w, width):
    return jnp.pad(w, ((0, 0), (0, width - w.shape[1])))


def _block_diag(w):
    n, d, _ = w.shape
    eye = jnp.eye(n, dtype=w.dtype)
    return (eye[:, None, :, None] * w[:, :, None, :]).reshape(n * d, n * d)


def _tile_of(n, pref):
    t = min(n, pref)
    assert n % t == 0
    return t


def kernel(x, e_norm_mix, e_w_in, e_sink, e_w_pool, e_pool_scale, e_w_out, o_norm_mix, o_w_in, o_g_cq, o_w_uq,
           o_g_ckv, o_w_ukv, o_conv_w, o_conv_b, o_lru_wa, o_lru_ba, o_lru_wx, o_lru_bx, o_lru_lambda, o_w_out,
           norm_mlp, w_mlp1, w_mlp2, final_norm):
    batch, seq, D = x.shape
    T = batch * seq
    depth = norm_mlp.shape[0]
    x2 = x.reshape(T, D)

    tm_prep = _tile_of(seq, 512)
    tq_even = _tile_of(seq, 256)
    tm_mlp = _tile_of(T, 1024)
    tf_mlp = _tile_of(w_mlp1.shape[2], 512)
    tq_flash = _tile_of(seq, 512)
    tk_flash = _tile_of(seq, 2048)
    tm_lru = _tile_of(seq, 256)
    tm_out = _tile_of(seq, 512)

    cos_a, sin_a = _rope_tables(seq, A_HEAD_DIM // 2, 0, LANES)
    cos_c, sin_c = _rope_tables(seq, C_ROPE // 2, C_NOPE, C_NOPE + C_ROPE)
    q_scale_c = (C_NOPE + C_ROPE) ** -0.5 * LOG2E
    gf = final_norm.reshape(1, D)

    for layer in range(depth):
        li = layer // 2
        if layer % 2 == 0:
            w = e_w_in[li]
            kq = A_WIDTH
            k0 = w[:, kq:kq + A_HEAD_DIM]
            k1 = w[:, kq + A_HEAD_DIM:kq + 2 * A_HEAD_DIM]
            vq = kq + 2 * A_HEAD_DIM
            v0 = w[:, vq:vq + A_HEAD_DIM]
            v1 = w[:, vq + A_HEAD_DIM:vq + 2 * A_HEAD_DIM]
            w_in = jnp.concatenate([w[:, :kq], k0, k0, k1, k1, v0, v0, v1, v1, w[:, vq + 2 * A_HEAD_DIM:]],
                                   axis=1).astype(BF16)
            q, k, v, u = _even_prep(x2, e_norm_mix[li].reshape(1, D), w_in, cos_a, sin_a, seq, tm_prep)
            x2 = _even_mix(x2, q, k, v, u, e_sink[li], e_w_pool[li].astype(BF16), e_pool_scale[li].reshape(1, -1),
                           e_w_out[li].astype(BF16), seq, tq_even)
        else:
            w = o_w_in[li]
            i1 = C_Q_RANK
            i2 = i1 + C_KV_RANK
            i3 = i2 + C_ROPE
            kr_slab = jnp.pad(w[:, i2:i3], ((0, 0), (C_NOPE, LANES - C_NOPE - C_ROPE)))
            w_in = jnp.concatenate([w[:, :i2], kr_slab, w[:, i3:]], axis=1).astype(BF16)
            wuq = o_w_uq[li].reshape(C_Q_RANK, C_HEADS, C_NOPE + C_ROPE)
            wuq = jnp.pad(wuq, ((0, 0), (0, 0), (0, LANES - C_NOPE - C_ROPE))).reshape(C_Q_RANK, C_HEADS * LANES)
            wukv = o_w_ukv[li].reshape(C_KV_RANK, C_HEADS, C_NOPE + C_VDIM)
            wk = jnp.pad(wukv[:, :, :C_NOPE], ((0, 0), (0, 0), (0, LANES - C_NOPE)))
            wv = jnp.pad(wukv[:, :, C_NOPE:], ((0, 0), (0, 0), (0, LANES - C_VDIM)))
            wukv = jnp.concatenate([wk.reshape(C_KV_RANK, -1), wv.reshape(C_KV_RANK, -1)], axis=1)
            q, k, v, xr, xg = _odd_prep(x2, o_norm_mix[li].reshape(1, D), w_in, o_g_cq[li].reshape(1, -1),
                                        wuq.astype(BF16), o_g_ckv[li].reshape(1, -1), wukv.astype(BF16),
                                        cos_c, sin_c, seq, tm_prep, q_scale_c)
            n_chunks = seq // tk_flash
            vt = v[:, :, :V_ROWS].reshape(C_HEADS, batch, n_chunks, tk_flash, V_ROWS).swapaxes(3, 4)
            vt = vt.reshape(C_HEADS, batch * n_chunks, V_ROWS, tk_flash)
            yc = _flash(q, k, vt, batch, seq, tq_flash, tk_flash)

            def gate_w(d):
                return jnp.concatenate([_block_diag(o_lru_wa[li, d]), _block_diag(o_lru_wx[li, d])], axis=1).astype(BF16)

            def gate_b(d):
                return jnp.concatenate([o_lru_ba[li, d], o_lru_bx[li, d]]).reshape(1, -1)

            hf, hb = _lru(xr, o_conv_w[li], o_conv_b[li].reshape(1, -1), gate_w(0), gate_b(0), gate_w(1), gate_b(1),
                          o_lru_lambda[li], batch, seq, tm_lru)
            wo = o_w_out[li]
            wc = wo[:C_HEADS * C_VDIM].reshape(C_HEADS, C_VDIM, D)
            wc = jnp.pad(wc, ((0, 0), (0, LANES - C_VDIM), (0, 0))).astype(BF16)
            x2 = _odd_out(x2, yc, hf, hb, xg, wc, wo[C_HEADS * C_VDIM:].astype(BF16), tm_out)
        x2 = _mlp(x2, norm_mlp[layer].reshape(1, D), w_mlp1[layer].astype(BF16), w_mlp2[layer].astype(BF16), gf,
                  tm_mlp, tf_mlp, final_norm=(layer == depth - 1))
    return x2.reshape(batch, seq, D)
```

```python
import functools
import math

import jax
import jax.numpy as jnp
from jax import lax
from jax.experimental import pallas as pl
from jax.experimental.pallas import tpu as pltpu

F32 = jnp.float32
BF16 = jnp.bfloat16

EPS = 1e-6
ROPE_THETA = 10000.0
LANES = 128
SUBLANES = 8
MXU_DIM = 256
VMEM_LIMIT = 48 * 1024 * 1024

A_HEADS = 8
A_KV_HEADS = 2
A_HEAD_DIM = 64
A_WINDOW = 128
A_WIDTH = A_HEADS * A_HEAD_DIM
POOL_WINDOWS = (2, 4, 8, 16)
B_GROUP_DIM = 128
B_WIDTH = B_GROUP_DIM * len(POOL_WINDOWS)
POOL_HALO = 8
C_HEADS = 8
C_NOPE = 64
C_ROPE = 32
C_VDIM = 64
C_Q_RANK = 256
C_KV_RANK = 128
V_ROWS = 80
D_WIDTH = 512
D_BLOCKS = 8
D_BLOCK_DIM = 64
CONV_WIDTH = 4
LRU_C = 8.0

NEG_BIG = -1e30
LOG2E = math.log2(math.e)


def _params(*sem):
    return pltpu.CompilerParams(dimension_semantics=sem, vmem_limit_bytes=VMEM_LIMIT)


def _rms(x, g):
    return x * lax.rsqrt(jnp.mean(x * x, axis=-1, keepdims=True) + EPS) * g


def _rope_slab(xs, cos, sin, half):
    lane = lax.broadcasted_iota(jnp.int32, (1, LANES), 1)
    first = (lane % (2 * half)) < half
    swapped = jnp.where(first, pltpu.roll(xs, LANES - half, 1), pltpu.roll(xs, half, 1))
    return xs * cos + swapped * sin


def _even_prep_kernel(x_ref, g_ref, w_ref, cos_ref, sin_ref, q_ref, k_ref, v_ref, u_ref):
    h = _rms(x_ref[...], g_ref[...]).astype(BF16)
    z = jnp.dot(h, w_ref[...], preferred_element_type=F32)
    cos = cos_ref[...]
    sin = sin_ref[...]
    n_q = A_WIDTH // LANES
    for c in range(n_q):
        r = _rope_slab(z[:, c * LANES:(c + 1) * LANES], cos, sin, A_HEAD_DIM // 2)
        q_ref[:, c * LANES:(c + 1) * LANES] = (r * (A_HEAD_DIM ** -0.5)).astype(BF16)
    for c in range(A_KV_HEADS):
        lo = A_WIDTH + c * LANES
        k_ref[:, c * LANES:(c + 1) * LANES] = _rope_slab(z[:, lo:lo + LANES], cos, sin, A_HEAD_DIM // 2).astype(BF16)
    lo = A_WIDTH + A_KV_HEADS * LANES
    v_ref[...] = z[:, lo:lo + A_KV_HEADS * LANES].astype(BF16)
    u_ref[...] = z[:, lo + A_KV_HEADS * LANES:]


def _even_prep(x2, g, w, cos, sin, seq, tm):
    T, D = x2.shape
    n_in = w.shape[1]
    kvw = A_KV_HEADS * LANES
    nt_seq = seq // tm
    tok = lambda i: (i, 0)
    tab = lambda i: (i % nt_seq, 0)
    fixed = lambda i: (0, 0)
    return pl.pallas_call(
        _even_prep_kernel,
        grid=(T // tm,),
        in_specs=[pl.BlockSpec((tm, D), tok), pl.BlockSpec((1, D), fixed), pl.BlockSpec((D, n_in), fixed),
                  pl.BlockSpec((tm, LANES), tab), pl.BlockSpec((tm, LANES), tab)],
        out_specs=[pl.BlockSpec((tm, A_WIDTH), tok), pl.BlockSpec((tm, kvw), tok),
                   pl.BlockSpec((tm, kvw), tok), pl.BlockSpec((tm, B_WIDTH), tok)],
        out_shape=[jax.ShapeDtypeStruct((T, A_WIDTH), BF16), jax.ShapeDtypeStruct((T, kvw), BF16),
                   jax.ShapeDtypeStruct((T, kvw), BF16), jax.ShapeDtypeStruct((T, B_WIDTH), F32)],
        compiler_params=_params("parallel"),
        name="even_prep",
    )(x2, g, w, cos, sin)


def _even_mix_kernel(sink_ref, x_ref, q_ref, kp_ref, kc_ref, kn_ref, vp_ref, vc_ref, vn_ref,
                     up_ref, uc_ref, un_ref, wpool_ref, pscale_ref, wout_ref, o_ref, uext_ref, *, seq, tq):
    i = pl.program_id(0)
    nt_seq = seq // tq
    pos0 = (i % nt_seq) * tq
    at_start = pos0 == 0
    at_end = pos0 + tq == seq
    nk = tq + 2 * A_WINDOW

    kd = jnp.concatenate([kp_ref[...], kc_ref[...], kn_ref[...]], axis=0)
    vd = jnp.concatenate([vp_ref[...], vc_ref[...], vn_ref[...]], axis=0)
    r = lax.broadcasted_iota(jnp.int32, (tq, nk), 0)
    c = lax.broadcasted_iota(jnp.int32, (tq, nk), 1)
    kpos = pos0 - A_WINDOW + c
    valid = (jnp.abs(c - A_WINDOW - r) <= A_WINDOW) & (kpos >= 0) & (kpos < seq)
    valid2 = jnp.concatenate([valid, valid], axis=0)
    lane = lax.broadcasted_iota(jnp.int32, (1, LANES), 1)
    low = lane < A_HEAD_DIM
    row2 = lax.broadcasted_iota(jnp.int32, (2 * tq, 1), 0)
    ya = []
    for j in range(A_HEADS // 2):
        g = (2 * j) // (A_HEADS // A_KV_HEADS)
        qp = q_ref[:, j * LANES:(j + 1) * LANES]
        zero = jnp.zeros_like(qp)
        lhs = jnp.concatenate([jnp.where(low, qp, zero), jnp.where(low, zero, qp)], axis=0)
        s = lax.dot_general(lhs, kd[:, g * LANES:(g + 1) * LANES], (((1,), (1,)), ((), ())),
                            preferred_element_type=F32)
        s = jnp.where(valid2, s, NEG_BIG)
        sk = jnp.where(row2 < tq, sink_ref[2 * j], sink_ref[2 * j + 1])
        m = jnp.maximum(jnp.max(s, axis=1, keepdims=True), sk)
        p = jnp.exp(s - m)
        den = jnp.sum(p, axis=1, keepdims=True) + jnp.exp(sk - m)
        o = jnp.dot(p.astype(BF16), vd[:, g * LANES:(g + 1) * LANES], preferred_element_type=F32) / den
        ya.append(jnp.where(low, o[:tq], o[tq:]))

    uext_ref[0:POOL_HALO, :] = jnp.where(at_start, 0.0, up_ref[...])
    uext_ref[POOL_HALO:POOL_HALO + tq, :] = uc_ref[...]
    uext_ref[POOL_HALO + tq:, :] = jnp.where(at_end, 0.0, un_ref[...])
    t = pos0 + lax.broadcasted_iota(jnp.int32, (tq, 1), 0)
    yb = []
    for gi, w in enumerate(POOL_WINDOWS):
        half = w // 2
        cols = slice(gi * B_GROUP_DIM, (gi + 1) * B_GROUP_DIM)
        win = uext_ref[POOL_HALO - half:POOL_HALO - half + tq, cols]
        for off in range(-half + 1, half):
            win = win + uext_ref[POOL_HALO + off:POOL_HALO + off + tq, cols]
        cnt = (jnp.minimum(t + half, seq) - jnp.maximum(t - half, 0)).astype(F32)
        d = win / cnt - uc_ref[:, cols]
        y = jnp.dot(d.astype(BF16), wpool_ref[gi], preferred_element_type=F32)
        yb.append(y * pscale_ref[:, cols])

    ycat = jnp.concatenate(ya + yb, axis=1).astype(BF16)
    o_ref[...] = x_ref[...] + jnp.dot(ycat, wout_ref[...], preferred_element_type=F32)


def _even_mix(x2, q, k, v, u, sink, wpool, pscale, wout, seq, tq):
    T, D = x2.shape
    kvw = k.shape[1]
    nt_seq = seq // tq
    wb = tq // A_WINDOW
    hb = tq // POOL_HALO
    n_wb = T // A_WINDOW
    n_hb = T // POOL_HALO
    tok = lambda i: (i, 0)
    prev_w = lambda i: (jnp.maximum(i * wb - 1, 0), 0)
    next_w = lambda i: (jnp.minimum((i + 1) * wb, n_wb - 1), 0)
    prev_h = lambda i: (jnp.maximum(i * hb - 1, 0), 0)
    next_h = lambda i: (jnp.minimum((i + 1) * hb, n_hb - 1), 0)
    fixed2 = lambda i: (0, 0)
    fixed3 = lambda i: (0, 0, 0)
    return pl.pallas_call(
        functools.partial(_even_mix_kernel, seq=seq, tq=tq),
        grid=(T // tq,),
        in_specs=[pl.BlockSpec(memory_space=pltpu.SMEM),
                  pl.BlockSpec((tq, D), tok),
                  pl.BlockSpec((tq, A_WIDTH), tok),
                  pl.BlockSpec((A_WINDOW, kvw), prev_w), pl.BlockSpec((tq, kvw), tok), pl.BlockSpec((A_WINDOW, kvw), next_w),
                  pl.BlockSpec((A_WINDOW, kvw), prev_w), pl.BlockSpec((tq, kvw), tok), pl.BlockSpec((A_WINDOW, kvw), next_w),
                  pl.BlockSpec((POOL_HALO, B_WIDTH), prev_h), pl.BlockSpec((tq, B_WIDTH), tok), pl.BlockSpec((POOL_HALO, B_WIDTH), next_h),
                  pl.BlockSpec(wpool.shape, fixed3), pl.BlockSpec((1, B_WIDTH), fixed2), pl.BlockSpec((D, D), fixed2)],
        out_specs=pl.BlockSpec((tq, D), tok),
        out_shape=jax.ShapeDtypeStruct((T, D), F32),
        scratch_shapes=[pltpu.VMEM((tq + 2 * POOL_HALO, B_WIDTH), F32)],
        compiler_params=_params("parallel"),
        name="even_mix",
    )(sink, x2, q, k, k, k, v, v, v, u, u, u, wpool, pscale, wout)


def _mlp_kernel(x_ref, g_ref, w1_ref, w2_ref, gf_ref, o_ref, h_ref, *, final_norm):
    f = pl.program_id(1)

    @pl.when(f == 0)
    def _():
        x = x_ref[...]
        h_ref[...] = _rms(x, g_ref[...]).astype(BF16)
        o_ref[...] = x

    u = jnp.maximum(jnp.dot(h_ref[...], w1_ref[...], preferred_element_type=F32), 0.0)
    o_ref[...] += jnp.dot((u * u).astype(BF16), w2_ref[...], preferred_element_type=F32)

    if final_norm:
        @pl.when(f == pl.num_programs(1) - 1)
        def _():
            o_ref[...] = _rms(o_ref[...], gf_ref[...])


def _mlp(x2, g, w1, w2, gf, tm, tf, final_norm):
    T, D = x2.shape
    F = w1.shape[1]
    return pl.pallas_call(
        functools.partial(_mlp_kernel, final_norm=final_norm),
        grid=(T // tm, F // tf),
        in_specs=[pl.BlockSpec((tm, D), lambda i, f: (i, 0)), pl.BlockSpec((1, D), lambda i, f: (0, 0)),
                  pl.BlockSpec((D, tf), lambda i, f: (0, f)), pl.BlockSpec((tf, D), lambda i, f: (f, 0)),
                  pl.BlockSpec((1, D), lambda i, f: (0, 0))],
        out_specs=pl.BlockSpec((tm, D), lambda i, f: (i, 0)),
        out_shape=jax.ShapeDtypeStruct((T, D), F32),
        scratch_shapes=[pltpu.VMEM((tm, D), BF16)],
        compiler_params=_params("parallel", "arbitrary"),
        name="mlp",
    )(x2, g, w1, w2, gf)


def _odd_prep_kernel(x_ref, g_ref, w_ref, gq_ref, wuq_ref, gkv_ref, wukv_ref, cos_ref, sin_ref,
                     q_ref, k_ref, v_ref, xr_ref, xg_ref, *, q_scale):
    h = _rms(x_ref[...], g_ref[...]).astype(BF16)
    z = jnp.dot(h, w_ref[...], preferred_element_type=F32)
    cos = cos_ref[...]
    sin = sin_ref[...]
    o1 = C_Q_RANK
    o2 = o1 + C_KV_RANK
    o3 = o2 + LANES
    cq = _rms(z[:, :o1], gq_ref[...]).astype(BF16)
    ckv = _rms(z[:, o1:o2], gkv_ref[...]).astype(BF16)
    kr = _rope_slab(z[:, o2:o3], cos, sin, C_ROPE // 2)
    q = jnp.dot(cq, wuq_ref[...], preferred_element_type=F32)
    kv = jnp.dot(ckv, wukv_ref[...], preferred_element_type=F32)
    lane = lax.broadcasted_iota(jnp.int32, (1, LANES), 1)
    ones_col = (lane == C_VDIM).astype(F32)
    for hd in range(C_HEADS):
        qs = _rope_slab(q[:, hd * LANES:(hd + 1) * LANES], cos, sin, C_ROPE // 2)
        q_ref[hd] = (qs * q_scale).astype(BF16)
        k_ref[hd] = (kv[:, hd * LANES:(hd + 1) * LANES] + kr).astype(BF16)
        vo = (C_HEADS + hd) * LANES
        v_ref[hd] = (kv[:, vo:vo + LANES] + ones_col).astype(BF16)
    xr_ref[...] = z[:, o3:o3 + D_WIDTH]
    xg_ref[...] = z[:, o3 + D_WIDTH:]


def _odd_prep(x2, g, w, gq, wuq, gkv, wukv, cos, sin, seq, tm, q_scale):
    T, D = x2.shape
    nt_seq = seq // tm
    tok = lambda i: (i, 0)
    tab = lambda i: (i % nt_seq, 0)
    fixed = lambda i: (0, 0)
    head = lambda i: (0, i, 0)
    hshape = jax.ShapeDtypeStruct((C_HEADS, T, LANES), BF16)
    return pl.pallas_call(
        functools.partial(_odd_prep_kernel, q_scale=q_scale),
        grid=(T // tm,),
        in_specs=[pl.BlockSpec((tm, D), tok), pl.BlockSpec((1, D), fixed), pl.BlockSpec(w.shape, fixed),
                  pl.BlockSpec((1, C_Q_RANK), fixed), pl.BlockSpec(wuq.shape, fixed),
                  pl.BlockSpec((1, C_KV_RANK), fixed), pl.BlockSpec(wukv.shape, fixed),
                  pl.BlockSpec((tm, LANES), tab), pl.BlockSpec((tm, LANES), tab)],
        out_specs=[pl.BlockSpec((C_HEADS, tm, LANES), head), pl.BlockSpec((C_HEADS, tm, LANES), head),
                   pl.BlockSpec((C_HEADS, tm, LANES), head),
                   pl.BlockSpec((tm, D_WIDTH), tok), pl.BlockSpec((tm, D_WIDTH), tok)],
        out_shape=[hshape, hshape, hshape,
                   jax.ShapeDtypeStruct((T, D_WIDTH), F32), jax.ShapeDtypeStruct((T, D_WIDTH), F32)],
        compiler_params=_params("parallel"),
        name="odd_prep",
    )(x2, g, w, gq, wuq, gkv, wukv, cos, sin)


def _flash_kernel(q_ref, k_ref, vt_ref, o_ref, s_ref, p_ref, acc_ref, *, tq, tk):
    seq = k_ref.shape[0]
    n_chunks = seq // tk
    n_steps = (seq // tq) * n_chunks

    def scores(g):
        tile = g // n_chunks
        q = q_ref[pl.ds(pl.multiple_of(tile * tq, tq), tq), :]
        kc = k_ref[pl.ds(pl.multiple_of((g % n_chunks) * tk, tk), tk), :]
        st = lax.dot_general(kc, q, (((1,), (1,)), ((), ())), preferred_element_type=F32)
        s_ref[...] = st
        return jnp.max(st, axis=0, keepdims=True)

    def values(g, alpha):
        acc_ref[...] = alpha * acc_ref[...] + jnp.dot(vt_ref[g % n_chunks], p_ref[...],
                                                       preferred_element_type=F32)

    def finish_tile(tile):
        acc = acc_ref[...]
        out = acc[:C_VDIM] / acc[C_VDIM:C_VDIM + 1, :]
        out = jnp.concatenate([out, jnp.zeros((LANES - C_VDIM, tq), F32)], axis=0)
        o_ref[pl.ds(pl.multiple_of(tile * tq, tq), tq), :] = out.T.astype(BF16)
        acc_ref[...] = jnp.zeros_like(acc_ref)

    p_ref[...] = jnp.zeros_like(p_ref)
    acc_ref[...] = jnp.zeros_like(acc_ref)
    mx0 = scores(0)

    def body(g, carry):
        m, mx, alpha = carry
        first = g % n_chunks == 0
        values(jnp.maximum(g - 1, 0), alpha)
        m_old = jnp.where(first, NEG_BIG, m)
        m_new = jnp.maximum(m_old, mx)
        p_ref[...] = jnp.exp2(s_ref[...] - m_new).astype(BF16)
        mx_next = scores(jnp.minimum(g + 1, n_steps - 1))

        @pl.when(jnp.logical_and(first, g > 0))
        def _():
            finish_tile(g // n_chunks - 1)

        return m_new, mx_next, jnp.exp2(m_old - m_new)

    m0 = jnp.full((1, tq), NEG_BIG, F32)
    _, _, alpha = lax.fori_loop(0, n_steps, body, (m0, mx0, jnp.ones((1, tq), F32)))
    values(n_steps - 1, alpha)
    finish_tile(seq // tq - 1)


def _flash(q, k, vt, batch, seq, tq, tk):
    H, T, _ = q.shape
    n_chunks = seq // tk
    per_head = lambda b, h: (h, b, 0)
    return pl.pallas_call(
        functools.partial(_flash_kernel, tq=tq, tk=tk),
        grid=(batch, H),
        in_specs=[pl.BlockSpec((None, seq, LANES), per_head),
                  pl.BlockSpec((None, seq, LANES), per_head),
                  pl.BlockSpec((None, n_chunks, V_ROWS, tk), lambda b, h: (h, b, 0, 0))],
        out_specs=pl.BlockSpec((None, seq, LANES), per_head),
        out_shape=jax.ShapeDtypeStruct((H, T, LANES), BF16),
        scratch_shapes=[pltpu.VMEM((tk, tq), F32), pltpu.VMEM((tk, tq), BF16), pltpu.VMEM((V_ROWS, tq), F32)],
        compiler_params=_params("parallel", "parallel"),
        name="mla_flash",
    )(q, k, vt)


def _scan_tile(a, b, carry, reverse):
    rows, width = a.shape
    groups = rows // SUBLANES
    a = a.reshape(groups, SUBLANES, width)
    b = b.reshape(groups, SUBLANES, width)
    sub = lax.broadcasted_iota(jnp.int32, (1, SUBLANES, 1), 1)
    k = 1
    while k < SUBLANES:
        shift = (SUBLANES - k) if reverse else k
        ok = (sub < SUBLANES - k) if reverse else (sub >= k)
        a_s = pltpu.roll(a, shift, 1)
        b_s = pltpu.roll(b, shift, 1)
        b = jnp.where(ok, a * b_s + b, b)
        a = jnp.where(ok, a * a_s, a)
        k *= 2
    order = range(groups - 1, -1, -1) if reverse else range(groups)
    edge = 0 if reverse else SUBLANES - 1
    out = [None] * groups
    for gi in order:
        hg = a[gi] * carry + b[gi]
        out[gi] = hg
        carry = hg[edge:edge + 1]
    return jnp.concatenate(out, axis=0), carry


def _lru_gates(xp_ref, xc_ref, xn_ref, lo_edge, hi_edge, cw_ref, cb_ref, wg_ref, bg_ref, lam_row, ext_ref):
    tm = xc_ref.shape[0]
    ext_ref[0:SUBLANES, :] = jnp.where(lo_edge, 0.0, xp_ref[...])
    ext_ref[SUBLANES:SUBLANES + tm, :] = xc_ref[...]
    ext_ref[SUBLANES + tm:, :] = jnp.where(hi_edge, 0.0, xn_ref[...])
    left = CONV_WIDTH // 2
    xc = cb_ref[...] + cw_ref[0:1, :] * ext_ref[SUBLANES - left:SUBLANES - left + tm, :]
    for j in range(1, CONV_WIDTH):
        xc = xc + cw_ref[j:j + 1, :] * ext_ref[SUBLANES - left + j:SUBLANES - left + j + tm, :]
    gates = jnp.dot(xc.astype(BF16), wg_ref[...], preferred_element_type=F32) + bg_ref[...]
    gates = 1.0 / (1.0 + jnp.exp(-gates))
    r = gates[:, :D_WIDTH]
    ig = gates[:, D_WIDTH:]
    neg = -lam_row
    softplus = jnp.maximum(neg, 0.0) + jnp.log1p(jnp.exp(-jnp.abs(neg)))
    log_a = (-LRU_C) * r * softplus
    a = jnp.exp(log_a)
    th = jnp.tanh(log_a)
    b = jnp.sqrt(-2.0 * th / (1.0 - th)) * (ig * xc)
    return a, b


def _lru_kernel(fp_ref, fc_ref, fn_ref, bp_ref, bc_ref, bn_ref, cw_ref, cb_ref, wgf_ref, bgf_ref,
                wgb_ref, bgb_ref, lam_ref, hf_ref, hb_ref, ext_ref, carry_ref):
    i = pl.program_id(1)
    last = pl.num_programs(1) - 1

    @pl.when(i == 0)
    def _():
        carry_ref[...] = jnp.zeros_like(carry_ref)

    a, b = _lru_gates(fp_ref, fc_ref, fn_ref, i == 0, i == last, cw_ref, cb_ref, wgf_ref, bgf_ref,
                      lam_ref[0:1, :], ext_ref)
    h, cf = _scan_tile(a, b, carry_ref[0:1, :], reverse=False)
    hf_ref[...] = h
    carry_ref[0:1, :] = cf

    a, b = _lru_gates(bp_ref, bc_ref, bn_ref, i == last, i == 0, cw_ref, cb_ref, wgb_ref, bgb_ref,
                      lam_ref[1:2, :], ext_ref)
    h, cb = _scan_tile(a, b, carry_ref[1:2, :], reverse=True)
    hb_ref[...] = h
    carry_ref[1:2, :] = cb


def _lru(xr, cw, cb, wgf, bgf, wgb, bgb, lam, batch, seq, tm):
    T, W = xr.shape
    nt = seq // tm
    hb = tm // SUBLANES
    n_hb = T // SUBLANES
    f_tok = lambda b, i: (b * nt + i, 0)
    b_tok = lambda b, i: (b * nt + nt - 1 - i, 0)
    f_prev = lambda b, i: (jnp.maximum((b * nt + i) * hb - 1, 0), 0)
    f_next = lambda b, i: (jnp.minimum((b * nt + i + 1) * hb, n_hb - 1), 0)
    b_prev = lambda b, i: (jnp.maximum((b * nt + nt - 1 - i) * hb - 1, 0), 0)
    b_next = lambda b, i: (jnp.minimum((b * nt + nt - i) * hb, n_hb - 1), 0)
    fixed = lambda b, i: (0, 0)
    halo = (SUBLANES, W)
    return pl.pallas_call(
        _lru_kernel,
        grid=(batch, nt),
        in_specs=[pl.BlockSpec(halo, f_prev), pl.BlockSpec((tm, W), f_tok), pl.BlockSpec(halo, f_next),
                  pl.BlockSpec(halo, b_prev), pl.BlockSpec((tm, W), b_tok), pl.BlockSpec(halo, b_next),
                  pl.BlockSpec(cw.shape, fixed), pl.BlockSpec(cb.shape, fixed),
                  pl.BlockSpec(wgf.shape, fixed), pl.BlockSpec(bgf.shape, fixed),
                  pl.BlockSpec(wgb.shape, fixed), pl.BlockSpec(bgb.shape, fixed),
                  pl.BlockSpec(lam.shape, fixed)],
        out_specs=[pl.BlockSpec((tm, W), f_tok), pl.BlockSpec((tm, W), b_tok)],
        out_shape=[jax.ShapeDtypeStruct((T, W), F32), jax.ShapeDtypeStruct((T, W), F32)],
        scratch_shapes=[pltpu.VMEM((tm + 2 * SUBLANES, W), F32), pltpu.VMEM((2, W), F32)],
        compiler_params=_params("arbitrary", "arbitrary"),
        name="rglru",
    )(xr, xr, xr, xr, xr, xr, cw, cb, wgf, bgf, wgb, bgb, lam)


def _gelu_tanh(x):
    return 0.5 * x * (1.0 + jnp.tanh(math.sqrt(2.0 / math.pi) * (x + 0.044715 * (x * x * x))))


def _odd_out_kernel(x_ref, yc_ref, hf_ref, hb_ref, xg_ref, wc_ref, wd_ref, o_ref):
    yd = ((hf_ref[...] + hb_ref[...]) * _gelu_tanh(xg_ref[...])).astype(BF16)
    acc = x_ref[...] + jnp.dot(yd, wd_ref[...], preferred_element_type=F32)
    for hd in range(C_HEADS):
        acc = acc + jnp.dot(yc_ref[hd], wc_ref[hd], preferred_element_type=F32)
    o_ref[...] = acc


def _odd_out(x2, yc, hf, hb, xg, wc, wd, tm):
    T, D = x2.shape
    tok = lambda i: (i, 0)
    return pl.pallas_call(
        _odd_out_kernel,
        grid=(T // tm,),
        in_specs=[pl.BlockSpec((tm, D), tok), pl.BlockSpec((C_HEADS, tm, LANES), lambda i: (0, i, 0)),
                  pl.BlockSpec((tm, D_WIDTH), tok), pl.BlockSpec((tm, D_WIDTH), tok), pl.BlockSpec((tm, D_WIDTH), tok),
                  pl.BlockSpec(wc.shape, lambda i: (0, 0, 0)), pl.BlockSpec(wd.shape, lambda i: (0, 0))],
        out_specs=pl.BlockSpec((tm, D), tok),
        out_shape=jax.ShapeDtypeStruct((T, D), F32),
        compiler_params=_params("parallel"),
        name="odd_out",
    )(x2, yc, hf, hb, xg, wc, wd)


def _rope_tables(seq, half, lane_lo, lane_hi):
    inv = ROPE_THETA ** (-jnp.arange(half, dtype=F32) / half)
    ang = jnp.arange(seq, dtype=F32)[:, None] * inv[None, :]
    lane = jnp.arange(LANES)
    idx = lane % half
    sign = jnp.where((lane % (2 * half)) < half, -1.0, 1.0).astype(F32)
    active = (lane >= lane_lo) & (lane < lane_hi)
    cos = jnp.where(active[None, :], jnp.cos(ang)[:, idx], 1.0)
    sin = jnp.where(active[None, :], jnp.sin(ang)[:, idx] * sign[None, :], 0.0)
    return cos, sin


def _pad_cols(w, width):
    return jnp.pad(w, ((0, 0), (0, width - w.shape[1])))


def _block_diag(w):
    n, d, _ = w.shape
    eye = jnp.eye(n, dtype=w.dtype)
    return (eye[:, None, :, None] * w[:, :, None, :]).reshape(n * d, n * d)


def _tile_of(n, pref):
    t = min(n, pref)
    assert n % t == 0
    return t


def kernel(x, e_norm_mix, e_w_in, e_sink, e_w_pool, e_pool_scale, e_w_out, o_norm_mix, o_w_in, o_g_cq, o_w_uq,
           o_g_ckv, o_w_ukv, o_conv_w, o_conv_b, o_lru_wa, o_lru_ba, o_lru_wx, o_lru_bx, o_lru_lambda, o_w_out,
           norm_mlp, w_mlp1, w_mlp2, final_norm):
    batch, seq, D = x.shape
    T = batch * seq
    depth = norm_mlp.shape[0]
    x2 = x.reshape(T, D)

    tm_prep = _tile_of(seq, 512)
    tq_even = _tile_of(seq, 256)
    tm_mlp = _tile_of(T, 1024)
    tf_mlp = _tile_of(w_mlp1.shape[2], 1024)
    tq_flash = _tile_of(seq, 512)
    tk_flash = _tile_of(seq, 4096)
    tm_lru = _tile_of(seq, 256)
    tm_out = _tile_of(seq, 512)

    cos_a, sin_a = _rope_tables(seq, A_HEAD_DIM // 2, 0, LANES)
    cos_c, sin_c = _rope_tables(seq, C_ROPE // 2, C_NOPE, C_NOPE + C_ROPE)
    q_scale_c = (C_NOPE + C_ROPE) ** -0.5 * LOG2E
    gf = final_norm.reshape(1, D)

    for layer in range(depth):
        li = layer // 2
        if layer % 2 == 0:
            w = e_w_in[li]
            kq = A_WIDTH
            k0 = w[:, kq:kq + A_HEAD_DIM]
            k1 = w[:, kq + A_HEAD_DIM:kq + 2 * A_HEAD_DIM]
            vq = kq + 2 * A_HEAD_DIM
            v0 = w[:, vq:vq + A_HEAD_DIM]
            v1 = w[:, vq + A_HEAD_DIM:vq + 2 * A_HEAD_DIM]
            w_in = jnp.concatenate([w[:, :kq], k0, k0, k1, k1, v0, v0, v1, v1, w[:, vq + 2 * A_HEAD_DIM:]],
                                   axis=1).astype(BF16)
            q, k, v, u = _even_prep(x2, e_norm_mix[li].reshape(1, D), w_in, cos_a, sin_a, seq, tm_prep)
            x2 = _even_mix(x2, q, k, v, u, e_sink[li], e_w_pool[li].astype(BF16), e_pool_scale[li].reshape(1, -1),
                           e_w_out[li].astype(BF16), seq, tq_even)
        else:
            w = o_w_in[li]
            i1 = C_Q_RANK
            i2 = i1 + C_KV_RANK
            i3 = i2 + C_ROPE
            kr_slab = jnp.pad(w[:, i2:i3], ((0, 0), (C_NOPE, LANES - C_NOPE - C_ROPE)))
            w_in = jnp.concatenate([w[:, :i2], kr_slab, w[:, i3:]], axis=1).astype(BF16)
            wuq = o_w_uq[li].reshape(C_Q_RANK, C_HEADS, C_NOPE + C_ROPE)
            wuq = jnp.pad(wuq, ((0, 0), (0, 0), (0, LANES - C_NOPE - C_ROPE))).reshape(C_Q_RANK, C_HEADS * LANES)
            wukv = o_w_ukv[li].reshape(C_KV_RANK, C_HEADS, C_NOPE + C_VDIM)
            wk = jnp.pad(wukv[:, :, :C_NOPE], ((0, 0), (0, 0), (0, LANES - C_NOPE)))
            wv = jnp.pad(wukv[:, :, C_NOPE:], ((0, 0), (0, 0), (0, LANES - C_VDIM)))
            wukv = jnp.concatenate([wk.reshape(C_KV_RANK, -1), wv.reshape(C_KV_RANK, -1)], axis=1)
            q, k, v, xr, xg = _odd_prep(x2, o_norm_mix[li].reshape(1, D), w_in, o_g_cq[li].reshape(1, -1),
                                        wuq.astype(BF16), o_g_ckv[li].reshape(1, -1), wukv.astype(BF16),
                                        cos_c, sin_c, seq, tm_prep, q_scale_c)
            n_chunks = seq // tk_flash
            vt = v[:, :, :V_ROWS].reshape(C_HEADS, batch, n_chunks, tk_flash, V_ROWS).swapaxes(3, 4)
            vt = vt.reshape(C_HEADS, batch * n_chunks, V_ROWS, tk_flash)
            yc = _flash(q, k, vt, batch, seq, tq_flash, tk_flash)

            def gate_w(d):
                return jnp.concatenate([_block_diag(o_lru_wa[li, d]), _block_diag(o_lru_wx[li, d])], axis=1).astype(BF16)

            def gate_b(d):
                return jnp.concatenate([o_lru_ba[li, d], o_lru_bx[li, d]]).reshape(1, -1)

            hf, hb = _lru(xr, o_conv_w[li], o_conv_b[li].reshape(1, -1), gate_w(0), gate_b(0), gate_w(1), gate_b(1),
                          o_lru_lambda[li], batch, seq, tm_lru)
            wo = o_w_out[li]
            wc = wo[:C_HEADS * C_VDIM].reshape(C_HEADS, C_VDIM, D)
            wc = jnp.pad(wc, ((0, 0), (0, LANES - C_VDIM), (0, 0))).astype(BF16)
            x2 = _odd_out(x2, yc, hf, hb, xg, wc, wo[C_HEADS * C_VDIM:].astype(BF16), tm_out)
        x2 = _mlp(x2, norm_mlp[layer].reshape(1, D), w_mlp1[layer].astype(BF16), w_mlp2[layer].astype(BF16), gf,
                  tm_mlp, tf_mlp, final_norm=(layer == depth - 1))
    return x2.reshape(batch, seq, D)
```

```python
import functools
import math

import jax
import jax.numpy as jnp
import numpy as np
from jax import lax
from jax.experimental import pallas as pl
from jax.experimental.pallas import tpu as pltpu

F32 = jnp.float32
BF16 = jnp.bfloat16

EPS = 1e-6
ROPE_THETA = 10000.0
LANES = 128
SUBLANES = 8
MXU_DIM = 256
VMEM_LIMIT = 48 * 1024 * 1024

A_HEADS = 8
A_KV_HEADS = 2
A_HEAD_DIM = 64
A_WINDOW = 128
A_WIDTH = A_HEADS * A_HEAD_DIM
POOL_WINDOWS = (2, 4, 8, 16)
B_GROUP_DIM = 128
B_WIDTH = B_GROUP_DIM * len(POOL_WINDOWS)
POOL_HALO = 8
C_HEADS = 8
C_NOPE = 64
C_ROPE = 32
C_VDIM = 64
C_Q_RANK = 256
C_KV_RANK = 128
V_ROWS = 80
D_WIDTH = 512
D_BLOCKS = 8
D_BLOCK_DIM = 64
CONV_WIDTH = 4
LRU_C = 8.0

NEG_BIG = -1e30
LOG2E = math.log2(math.e)


def _params(*sem):
    return pltpu.CompilerParams(dimension_semantics=sem, vmem_limit_bytes=VMEM_LIMIT)


def _rms(x, g):
    return x * lax.rsqrt(jnp.mean(x * x, axis=-1, keepdims=True) + EPS) * g


def _rope_slab(xs, cos, sin, half):
    lane = lax.broadcasted_iota(jnp.int32, (1, LANES), 1)
    first = (lane % (2 * half)) < half
    swapped = jnp.where(first, pltpu.roll(xs, LANES - half, 1), pltpu.roll(xs, half, 1))
    return xs * cos + swapped * sin


def _even_prep_kernel(x_ref, g_ref, w_ref, cos_ref, sin_ref, q_ref, k_ref, v_ref, u_ref):
    h = _rms(x_ref[...], g_ref[...]).astype(BF16)
    z = jnp.dot(h, w_ref[...], preferred_element_type=F32)
    cos = cos_ref[...]
    sin = sin_ref[...]
    n_q = A_WIDTH // LANES
    for c in range(n_q):
        r = _rope_slab(z[:, c * LANES:(c + 1) * LANES], cos, sin, A_HEAD_DIM // 2)
        q_ref[:, c * LANES:(c + 1) * LANES] = (r * (A_HEAD_DIM ** -0.5)).astype(BF16)
    for c in range(A_KV_HEADS):
        lo = A_WIDTH + c * LANES
        k_ref[:, c * LANES:(c + 1) * LANES] = _rope_slab(z[:, lo:lo + LANES], cos, sin, A_HEAD_DIM // 2).astype(BF16)
    lo = A_WIDTH + A_KV_HEADS * LANES
    v_ref[...] = z[:, lo:lo + A_KV_HEADS * LANES].astype(BF16)
    u_ref[...] = z[:, lo + A_KV_HEADS * LANES:]


def _even_prep(x2, g, w, cos, sin, seq, tm):
    T, D = x2.shape
    n_in = w.shape[1]
    kvw = A_KV_HEADS * LANES
    nt_seq = seq // tm
    tok = lambda i: (i, 0)
    tab = lambda i: (i % nt_seq, 0)
    fixed = lambda i: (0, 0)
    return pl.pallas_call(
        _even_prep_kernel,
        grid=(T // tm,),
        in_specs=[pl.BlockSpec((tm, D), tok), pl.BlockSpec((1, D), fixed), pl.BlockSpec((D, n_in), fixed),
                  pl.BlockSpec((tm, LANES), tab), pl.BlockSpec((tm, LANES), tab)],
        out_specs=[pl.BlockSpec((tm, A_WIDTH), tok), pl.BlockSpec((tm, kvw), tok),
                   pl.BlockSpec((tm, kvw), tok), pl.BlockSpec((tm, B_WIDTH), tok)],
        out_shape=[jax.ShapeDtypeStruct((T, A_WIDTH), BF16), jax.ShapeDtypeStruct((T, kvw), BF16),
                   jax.ShapeDtypeStruct((T, kvw), BF16), jax.ShapeDtypeStruct((T, B_WIDTH), F32)],
        compiler_params=_params("parallel"),
        name="even_prep",
    )(x2, g, w, cos, sin)


def _even_mix_kernel(sink_ref, x_ref, q_ref, kp_ref, kc_ref, kn_ref, vp_ref, vc_ref, vn_ref,
                     up_ref, uc_ref, un_ref, wpool_ref, pscale_ref, wout_ref, o_ref, uext_ref, *, seq, tq):
    i = pl.program_id(0)
    nt_seq = seq // tq
    pos0 = (i % nt_seq) * tq
    at_start = pos0 == 0
    at_end = pos0 + tq == seq
    nk = tq + 2 * A_WINDOW

    kd = jnp.concatenate([kp_ref[...], kc_ref[...], kn_ref[...]], axis=0)
    vd = jnp.concatenate([vp_ref[...], vc_ref[...], vn_ref[...]], axis=0)
    r = lax.broadcasted_iota(jnp.int32, (tq, nk), 0)
    c = lax.broadcasted_iota(jnp.int32, (tq, nk), 1)
    kpos = pos0 - A_WINDOW + c
    valid = (jnp.abs(c - A_WINDOW - r) <= A_WINDOW) & (kpos >= 0) & (kpos < seq)
    valid2 = jnp.concatenate([valid, valid], axis=0)
    lane = lax.broadcasted_iota(jnp.int32, (1, LANES), 1)
    low = lane < A_HEAD_DIM
    row2 = lax.broadcasted_iota(jnp.int32, (2 * tq, 1), 0)
    ya = []
    for j in range(A_HEADS // 2):
        g = (2 * j) // (A_HEADS // A_KV_HEADS)
        qp = q_ref[:, j * LANES:(j + 1) * LANES]
        zero = jnp.zeros_like(qp)
        lhs = jnp.concatenate([jnp.where(low, qp, zero), jnp.where(low, zero, qp)], axis=0)
        s = lax.dot_general(lhs, kd[:, g * LANES:(g + 1) * LANES], (((1,), (1,)), ((), ())),
                            preferred_element_type=F32)
        s = jnp.where(valid2, s, NEG_BIG)
        sk = jnp.where(row2 < tq, sink_ref[2 * j], sink_ref[2 * j + 1])
        m = jnp.maximum(jnp.max(s, axis=1, keepdims=True), sk)
        p = jnp.exp(s - m)
        den = jnp.sum(p, axis=1, keepdims=True) + jnp.exp(sk - m)
        o = jnp.dot(p.astype(BF16), vd[:, g * LANES:(g + 1) * LANES], preferred_element_type=F32) / den
        ya.append(jnp.where(low, o[:tq], o[tq:]))

    uext_ref[0:POOL_HALO, :] = jnp.where(at_start, 0.0, up_ref[...])
    uext_ref[POOL_HALO:POOL_HALO + tq, :] = uc_ref[...]
    uext_ref[POOL_HALO + tq:, :] = jnp.where(at_end, 0.0, un_ref[...])
    t = pos0 + lax.broadcasted_iota(jnp.int32, (tq, 1), 0)
    yb = []
    for gi, w in enumerate(POOL_WINDOWS):
        half = w // 2
        cols = slice(gi * B_GROUP_DIM, (gi + 1) * B_GROUP_DIM)
        win = uext_ref[POOL_HALO - half:POOL_HALO - half + tq, cols]
        for off in range(-half + 1, half):
            win = win + uext_ref[POOL_HALO + off:POOL_HALO + off + tq, cols]
        cnt = (jnp.minimum(t + half, seq) - jnp.maximum(t - half, 0)).astype(F32)
        d = win / cnt - uc_ref[:, cols]
        y = jnp.dot(d.astype(BF16), wpool_ref[gi], preferred_element_type=F32)
        yb.append(y * pscale_ref[:, cols])

    ycat = jnp.concatenate(ya + yb, axis=1).astype(BF16)
    o_ref[...] = x_ref[...] + jnp.dot(ycat, wout_ref[...], preferred_element_type=F32)


def _even_mix(x2, q, k, v, u, sink, wpool, pscale, wout, seq, tq):
    T, D = x2.shape
    kvw = k.shape[1]
    nt_seq = seq // tq
    wb = tq // A_WINDOW
    hb = tq // POOL_HALO
    n_wb = T // A_WINDOW
    n_hb = T // POOL_HALO
    tok = lambda i: (i, 0)
    prev_w = lambda i: (jnp.maximum(i * wb - 1, 0), 0)
    next_w = lambda i: (jnp.minimum((i + 1) * wb, n_wb - 1), 0)
    prev_h = lambda i: (jnp.maximum(i * hb - 1, 0), 0)
    next_h = lambda i: (jnp.minimum((i + 1) * hb, n_hb - 1), 0)
    fixed2 = lambda i: (0, 0)
    fixed3 = lambda i: (0, 0, 0)
    return pl.pallas_call(
        functools.partial(_even_mix_kernel, seq=seq, tq=tq),
        grid=(T // tq,),
        in_specs=[pl.BlockSpec(memory_space=pltpu.SMEM),
                  pl.BlockSpec((tq, D), tok),
                  pl.BlockSpec((tq, A_WIDTH), tok),
                  pl.BlockSpec((A_WINDOW, kvw), prev_w), pl.BlockSpec((tq, kvw), tok), pl.BlockSpec((A_WINDOW, kvw), next_w),
                  pl.BlockSpec((A_WINDOW, kvw), prev_w), pl.BlockSpec((tq, kvw), tok), pl.BlockSpec((A_WINDOW, kvw), next_w),
                  pl.BlockSpec((POOL_HALO, B_WIDTH), prev_h), pl.BlockSpec((tq, B_WIDTH), tok), pl.BlockSpec((POOL_HALO, B_WIDTH), next_h),
                  pl.BlockSpec(wpool.shape, fixed3), pl.BlockSpec((1, B_WIDTH), fixed2), pl.BlockSpec((D, D), fixed2)],
        out_specs=pl.BlockSpec((tq, D), tok),
        out_shape=jax.ShapeDtypeStruct((T, D), F32),
        scratch_shapes=[pltpu.VMEM((tq + 2 * POOL_HALO, B_WIDTH), F32)],
        compiler_params=_params("parallel"),
        name="even_mix",
    )(sink, x2, q, k, k, k, v, v, v, u, u, u, wpool, pscale, wout)


def _mlp_kernel(x_ref, g_ref, w1_ref, w2_ref, gf_ref, o_ref, h_ref, *, final_norm):
    f = pl.program_id(1)

    @pl.when(f == 0)
    def _():
        x = x_ref[...]
        h_ref[...] = _rms(x, g_ref[...]).astype(BF16)
        o_ref[...] = x

    u = jnp.maximum(jnp.dot(h_ref[...], w1_ref[...], preferred_element_type=F32), 0.0)
    o_ref[...] += jnp.dot((u * u).astype(BF16), w2_ref[...], preferred_element_type=F32)

    if final_norm:
        @pl.when(f == pl.num_programs(1) - 1)
        def _():
            o_ref[...] = _rms(o_ref[...], gf_ref[...])


def _mlp(x2, g, w1, w2, gf, tm, tf, final_norm):
    T, D = x2.shape
    F = w1.shape[1]
    return pl.pallas_call(
        functools.partial(_mlp_kernel, final_norm=final_norm),
        grid=(T // tm, F // tf),
        in_specs=[pl.BlockSpec((tm, D), lambda i, f: (i, 0)), pl.BlockSpec((1, D), lambda i, f: (0, 0)),
                  pl.BlockSpec((D, tf), lambda i, f: (0, f)), pl.BlockSpec((tf, D), lambda i, f: (f, 0)),
                  pl.BlockSpec((1, D), lambda i, f: (0, 0))],
        out_specs=pl.BlockSpec((tm, D), lambda i, f: (i, 0)),
        out_shape=jax.ShapeDtypeStruct((T, D), F32),
        scratch_shapes=[pltpu.VMEM((tm, D), BF16)],
        compiler_params=_params("parallel", "arbitrary"),
        name="mlp",
    )(x2, g, w1, w2, gf)


def _odd_prep_kernel(x_ref, g_ref, w_ref, gq_ref, wuq_ref, gkv_ref, wukv_ref, cos_ref, sin_ref,
                     q_ref, k_ref, vt_ref, xr_ref, xg_ref, *, q_scale):
    h = _rms(x_ref[...], g_ref[...]).astype(BF16)
    z = jnp.dot(h, w_ref[...], preferred_element_type=F32)
    cos = cos_ref[...]
    sin = sin_ref[...]
    o1 = C_Q_RANK
    o2 = o1 + C_KV_RANK
    o3 = o2 + LANES
    cq = _rms(z[:, :o1], gq_ref[...]).astype(BF16)
    ckv = _rms(z[:, o1:o2], gkv_ref[...]).astype(BF16)
    kr = _rope_slab(z[:, o2:o3], cos, sin, C_ROPE // 2)
    q = jnp.dot(cq, wuq_ref[...], preferred_element_type=F32)
    kv = jnp.dot(ckv, wukv_ref[...], preferred_element_type=F32)
    lane = lax.broadcasted_iota(jnp.int32, (1, LANES), 1)
    ones_col = (lane == C_VDIM).astype(F32)
    for hd in range(C_HEADS):
        qs = _rope_slab(q[:, hd * LANES:(hd + 1) * LANES], cos, sin, C_ROPE // 2)
        q_ref[hd] = (qs * q_scale).astype(BF16)
        k_ref[hd] = (kv[:, hd * LANES:(hd + 1) * LANES] + kr).astype(BF16)
        vo = (C_HEADS + hd) * LANES
        vt_ref[hd] = (kv[:, vo:vo + LANES] + ones_col).T[:V_ROWS].astype(BF16)
    xr_ref[...] = z[:, o3:o3 + D_WIDTH]
    xg_ref[...] = z[:, o3 + D_WIDTH:]


def _odd_prep(x2, g, w, gq, wuq, gkv, wukv, cos, sin, seq, tm, tk, q_scale):
    T, D = x2.shape
    nt_seq = seq // tm
    per_chunk = tk // tm
    tok = lambda i: (i, 0)
    tab = lambda i: (i % nt_seq, 0)
    fixed = lambda i: (0, 0)
    head = lambda i: (0, i, 0)
    hshape = jax.ShapeDtypeStruct((C_HEADS, T, LANES), BF16)
    return pl.pallas_call(
        functools.partial(_odd_prep_kernel, q_scale=q_scale),
        grid=(T // tm,),
        in_specs=[pl.BlockSpec((tm, D), tok), pl.BlockSpec((1, D), fixed), pl.BlockSpec(w.shape, fixed),
                  pl.BlockSpec((1, C_Q_RANK), fixed), pl.BlockSpec(wuq.shape, fixed),
                  pl.BlockSpec((1, C_KV_RANK), fixed), pl.BlockSpec(wukv.shape, fixed),
                  pl.BlockSpec((tm, LANES), tab), pl.BlockSpec((tm, LANES), tab)],
        out_specs=[pl.BlockSpec((C_HEADS, tm, LANES), head), pl.BlockSpec((C_HEADS, tm, LANES), head),
                   pl.BlockSpec((C_HEADS, None, V_ROWS, tm), lambda i: (0, i // per_chunk, 0, i % per_chunk)),
                   pl.BlockSpec((tm, D_WIDTH), tok), pl.BlockSpec((tm, D_WIDTH), tok)],
        out_shape=[hshape, hshape, jax.ShapeDtypeStruct((C_HEADS, T // tk, V_ROWS, tk), BF16),
                   jax.ShapeDtypeStruct((T, D_WIDTH), F32), jax.ShapeDtypeStruct((T, D_WIDTH), F32)],
        compiler_params=_params("parallel"),
        name="odd_prep",
    )(x2, g, w, gq, wuq, gkv, wukv, cos, sin)


def _flash_kernel(q_ref, k_ref, vt_ref, o_ref, s_ref, p_ref, acc_ref, *, tq, tk):
    seq = k_ref.shape[0]
    n_chunks = seq // tk
    n_steps = (seq // tq) * n_chunks

    def scores(g):
        tile = g // n_chunks
        q = q_ref[pl.ds(pl.multiple_of(tile * tq, tq), tq), :]
        kc = k_ref[pl.ds(pl.multiple_of((g % n_chunks) * tk, tk), tk), :]
        st = lax.dot_general(kc, q, (((1,), (1,)), ((), ())), preferred_element_type=F32)
        s_ref[...] = st
        return jnp.max(st, axis=0, keepdims=True)

    def values(g, alpha):
        acc_ref[...] = alpha * acc_ref[...] + jnp.dot(vt_ref[g % n_chunks], p_ref[...],
                                                       preferred_element_type=F32)

    def finish_tile(tile):
        acc = acc_ref[...]
        out = acc[:C_VDIM] / acc[C_VDIM:C_VDIM + 1, :]
        out = jnp.concatenate([out, jnp.zeros((LANES - C_VDIM, tq), F32)], axis=0)
        o_ref[pl.ds(pl.multiple_of(tile * tq, tq), tq), :] = out.T.astype(BF16)
        acc_ref[...] = jnp.zeros_like(acc_ref)

    p_ref[...] = jnp.zeros_like(p_ref)
    acc_ref[...] = jnp.zeros_like(acc_ref)
    mx0 = scores(0)

    def body(g, carry):
        m, mx, alpha = carry
        first = g % n_chunks == 0
        values(jnp.maximum(g - 1, 0), alpha)
        m_old = jnp.where(first, NEG_BIG, m)
        m_new = jnp.maximum(m_old, mx)
        p_ref[...] = jnp.exp2(s_ref[...] - m_new).astype(BF16)
        mx_next = scores(jnp.minimum(g + 1, n_steps - 1))

        @pl.when(jnp.logical_and(first, g > 0))
        def _():
            finish_tile(g // n_chunks - 1)

        return m_new, mx_next, jnp.exp2(m_old - m_new)

    m0 = jnp.full((1, tq), NEG_BIG, F32)
    _, _, alpha = lax.fori_loop(0, n_steps, body, (m0, mx0, jnp.ones((1, tq), F32)))
    values(n_steps - 1, alpha)
    finish_tile(seq // tq - 1)


def _flash(q, k, vt, batch, seq, tq, tk):
    H, T, _ = q.shape
    n_chunks = seq // tk
    per_head = lambda b, h: (h, b, 0)
    return pl.pallas_call(
        functools.partial(_flash_kernel, tq=tq, tk=tk),
        grid=(batch, H),
        in_specs=[pl.BlockSpec((None, seq, LANES), per_head),
                  pl.BlockSpec((None, seq, LANES), per_head),
                  pl.BlockSpec((None, n_chunks, V_ROWS, tk), lambda b, h: (h, b, 0, 0))],
        out_specs=pl.BlockSpec((None, seq, LANES), per_head),
        out_shape=jax.ShapeDtypeStruct((H, T, LANES), BF16),
        scratch_shapes=[pltpu.VMEM((tk, tq), F32), pltpu.VMEM((tk, tq), BF16), pltpu.VMEM((V_ROWS, tq), F32)],
        compiler_params=_params("parallel", "parallel"),
        name="mla_flash",
    )(q, k, vt)


def _scan_tile(a, b, carry, reverse):
    rows, width = a.shape
    groups = rows // SUBLANES
    a = a.reshape(groups, SUBLANES, width)
    b = b.reshape(groups, SUBLANES, width)
    sub = lax.broadcasted_iota(jnp.int32, (1, SUBLANES, 1), 1)
    k = 1
    while k < SUBLANES:
        shift = (SUBLANES - k) if reverse else k
        ok = (sub < SUBLANES - k) if reverse else (sub >= k)
        a_s = pltpu.roll(a, shift, 1)
        b_s = pltpu.roll(b, shift, 1)
        b = jnp.where(ok, a * b_s + b, b)
        a = jnp.where(ok, a * a_s, a)
        k *= 2
    order = range(groups - 1, -1, -1) if reverse else range(groups)
    edge = 0 if reverse else SUBLANES - 1
    out = [None] * groups
    for gi in order:
        hg = a[gi] * carry + b[gi]
        out[gi] = hg
        carry = hg[edge:edge + 1]
    return jnp.concatenate(out, axis=0), carry


def _lru_gates(xp_ref, xc_ref, xn_ref, lo_edge, hi_edge, cw_ref, cb_ref, wg_ref, bg_ref, lam_row, ext_ref):
    tm = xc_ref.shape[0]
    ext_ref[0:SUBLANES, :] = jnp.where(lo_edge, 0.0, xp_ref[...])
    ext_ref[SUBLANES:SUBLANES + tm, :] = xc_ref[...]
    ext_ref[SUBLANES + tm:, :] = jnp.where(hi_edge, 0.0, xn_ref[...])
    left = CONV_WIDTH // 2
    xc = cb_ref[...] + cw_ref[0:1, :] * ext_ref[SUBLANES - left:SUBLANES - left + tm, :]
    for j in range(1, CONV_WIDTH):
        xc = xc + cw_ref[j:j + 1, :] * ext_ref[SUBLANES - left + j:SUBLANES - left + j + tm, :]
    gates = jnp.dot(xc.astype(BF16), wg_ref[...], preferred_element_type=F32) + bg_ref[...]
    gates = 1.0 / (1.0 + jnp.exp(-gates))
    r = gates[:, :D_WIDTH]
    ig = gates[:, D_WIDTH:]
    neg = -lam_row
    softplus = jnp.maximum(neg, 0.0) + jnp.log1p(jnp.exp(-jnp.abs(neg)))
    log_a = (-LRU_C) * r * softplus
    a = jnp.exp(log_a)
    th = jnp.tanh(log_a)
    b = jnp.sqrt(-2.0 * th / (1.0 - th)) * (ig * xc)
    return a, b


def _lru_kernel(fp_ref, fc_ref, fn_ref, bp_ref, bc_ref, bn_ref, cw_ref, cb_ref, wgf_ref, bgf_ref,
                wgb_ref, bgb_ref, lam_ref, hf_ref, hb_ref, ext_ref, carry_ref):
    i = pl.program_id(1)
    last = pl.num_programs(1) - 1

    @pl.when(i == 0)
    def _():
        carry_ref[...] = jnp.zeros_like(carry_ref)

    a, b = _lru_gates(fp_ref, fc_ref, fn_ref, i == 0, i == last, cw_ref, cb_ref, wgf_ref, bgf_ref,
                      lam_ref[0:1, :], ext_ref)
    h, cf = _scan_tile(a, b, carry_ref[0:1, :], reverse=False)
    hf_ref[...] = h
    carry_ref[0:1, :] = cf

    a, b = _lru_gates(bp_ref, bc_ref, bn_ref, i == last, i == 0, cw_ref, cb_ref, wgb_ref, bgb_ref,
                      lam_ref[1:2, :], ext_ref)
    h, cb = _scan_tile(a, b, carry_ref[1:2, :], reverse=True)
    hb_ref[...] = h
    carry_ref[1:2, :] = cb


def _lru(xr, cw, cb, wgf, bgf, wgb, bgb, lam, batch, seq, tm):
    T, W = xr.shape
    nt = seq // tm
    hb = tm // SUBLANES
    n_hb = T // SUBLANES
    f_tok = lambda b, i: (b * nt + i, 0)
    b_tok = lambda b, i: (b * nt + nt - 1 - i, 0)
    f_prev = lambda b, i: (jnp.maximum((b * nt + i) * hb - 1, 0), 0)
    f_next = lambda b, i: (jnp.minimum((b * nt + i + 1) * hb, n_hb - 1), 0)
    b_prev = lambda b, i: (jnp.maximum((b * nt + nt - 1 - i) * hb - 1, 0), 0)
    b_next = lambda b, i: (jnp.minimum((b * nt + nt - i) * hb, n_hb - 1), 0)
    fixed = lambda b, i: (0, 0)
    halo = (SUBLANES, W)
    return pl.pallas_call(
        _lru_kernel,
        grid=(batch, nt),
        in_specs=[pl.BlockSpec(halo, f_prev), pl.BlockSpec((tm, W), f_tok), pl.BlockSpec(halo, f_next),
                  pl.BlockSpec(halo, b_prev), pl.BlockSpec((tm, W), b_tok), pl.BlockSpec(halo, b_next),
                  pl.BlockSpec(cw.shape, fixed), pl.BlockSpec(cb.shape, fixed),
                  pl.BlockSpec(wgf.shape, fixed), pl.BlockSpec(bgf.shape, fixed),
                  pl.BlockSpec(wgb.shape, fixed), pl.BlockSpec(bgb.shape, fixed),
                  pl.BlockSpec(lam.shape, fixed)],
        out_specs=[pl.BlockSpec((tm, W), f_tok), pl.BlockSpec((tm, W), b_tok)],
        out_shape=[jax.ShapeDtypeStruct((T, W), F32), jax.ShapeDtypeStruct((T, W), F32)],
        scratch_shapes=[pltpu.VMEM((tm + 2 * SUBLANES, W), F32), pltpu.VMEM((2, W), F32)],
        compiler_params=_params("arbitrary", "arbitrary"),
        name="rglru",
    )(xr, xr, xr, xr, xr, xr, cw, cb, wgf, bgf, wgb, bgb, lam)


def _gelu_tanh(x):
    return 0.5 * x * (1.0 + jnp.tanh(math.sqrt(2.0 / math.pi) * (x + 0.044715 * (x * x * x))))


def _odd_out_kernel(x_ref, yc_ref, hf_ref, hb_ref, xg_ref, w_ref, o_ref):
    lane = lax.broadcasted_iota(jnp.int32, (1, LANES), 1)
    low = lane < C_VDIM
    parts = []
    for hd in range(0, C_HEADS, 2):
        even = yc_ref[hd].astype(F32)
        odd = pltpu.roll(yc_ref[hd + 1].astype(F32), C_VDIM, 1)
        parts.append(jnp.where(low, even, odd).astype(BF16))
    parts.append(((hf_ref[...] + hb_ref[...]) * _gelu_tanh(xg_ref[...])).astype(BF16))
    ycat = jnp.concatenate(parts, axis=1)
    o_ref[...] = x_ref[...] + jnp.dot(ycat, w_ref[...], preferred_element_type=F32)


def _odd_out(x2, yc, hf, hb, xg, w, tm):
    T, D = x2.shape
    tok = lambda i: (i, 0)
    return pl.pallas_call(
        _odd_out_kernel,
        grid=(T // tm,),
        in_specs=[pl.BlockSpec((tm, D), tok), pl.BlockSpec((C_HEADS, tm, LANES), lambda i: (0, i, 0)),
                  pl.BlockSpec((tm, D_WIDTH), tok), pl.BlockSpec((tm, D_WIDTH), tok), pl.BlockSpec((tm, D_WIDTH), tok),
                  pl.BlockSpec(w.shape, lambda i: (0, 0))],
        out_specs=pl.BlockSpec((tm, D), tok),
        out_shape=jax.ShapeDtypeStruct((T, D), F32),
        compiler_params=_params("parallel"),
        name="odd_out",
    )(x2, yc, hf, hb, xg, w)


def _rope_tables(seq, half, lane_lo, lane_hi):
    inv = ROPE_THETA ** (-jnp.arange(half, dtype=F32) / half)
    ang = jnp.arange(seq, dtype=F32)[:, None] * inv[None, :]
    lane = jnp.arange(LANES)
    idx = lane % half
    sign = jnp.where((lane % (2 * half)) < half, -1.0, 1.0).astype(F32)
    active = (lane >= lane_lo) & (lane < lane_hi)
    cos = jnp.where(active[None, :], jnp.cos(ang)[:, idx], 1.0)
    sin = jnp.where(active[None, :], jnp.sin(ang)[:, idx] * sign[None, :], 0.0)
    return cos, sin


def _tile_of(n, pref):
    t = min(n, pref)
    assert n % t == 0
    return t


def kernel(x, e_norm_mix, e_w_in, e_sink, e_w_pool, e_pool_scale, e_w_out, o_norm_mix, o_w_in, o_g_cq, o_w_uq,
           o_g_ckv, o_w_ukv, o_conv_w, o_conv_b, o_lru_wa, o_lru_ba, o_lru_wx, o_lru_bx, o_lru_lambda, o_w_out,
           norm_mlp, w_mlp1, w_mlp2, final_norm):
    batch, seq, D = x.shape
    T = batch * seq
    depth = norm_mlp.shape[0]
    x2 = x.reshape(T, D)

    tm_prep = _tile_of(seq, 512)
    tq_even = _tile_of(seq, 256)
    tm_mlp = _tile_of(T, 1024)
    tf_mlp = _tile_of(w_mlp1.shape[2], 1024)
    tq_flash = _tile_of(seq, 512)
    tk_flash = _tile_of(seq, 4096)
    tm_lru = _tile_of(seq, 256)
    tm_out = _tile_of(seq, 512)

    cos_a, sin_a = _rope_tables(seq, A_HEAD_DIM // 2, 0, LANES)
    cos_c, sin_c = _rope_tables(seq, C_ROPE // 2, C_NOPE, C_NOPE + C_ROPE)
    q_scale_c = (C_NOPE + C_ROPE) ** -0.5 * LOG2E

    hd = A_HEAD_DIM
    kv_dup = np.concatenate([np.tile(A_WIDTH + j * hd + np.arange(hd), 2) for j in range(2 * A_KV_HEADS)])
    cols_e = np.concatenate([np.arange(A_WIDTH), kv_dup, np.arange(A_WIDTH + 2 * A_KV_HEADS * hd, e_w_in.shape[2])])
    e_w_in_p = jnp.take(e_w_in, cols_e, axis=2).astype(BF16)
    i2 = C_Q_RANK + C_KV_RANK
    i3 = i2 + C_ROPE
    zero_col = o_w_in.shape[2]
    cols_o = np.concatenate([np.arange(i2), np.full(C_NOPE, zero_col), np.arange(i2, i3),
                             np.full(LANES - C_NOPE - C_ROPE, zero_col), np.arange(i3, zero_col)])
    o_w_in_p = jnp.take(jnp.pad(o_w_in, ((0, 0), (0, 0), (0, 1))), cols_o, axis=2).astype(BF16)
    n_odd = o_w_in.shape[0]
    wuq = o_w_uq.reshape(n_odd, C_Q_RANK, C_HEADS, C_NOPE + C_ROPE)
    wuq = jnp.pad(wuq, ((0, 0), (0, 0), (0, 0), (0, LANES - C_NOPE - C_ROPE)))
    wuq = wuq.reshape(n_odd, C_Q_RANK, C_HEADS * LANES).astype(BF16)
    wukv = o_w_ukv.reshape(n_odd, C_KV_RANK, C_HEADS, C_NOPE + C_VDIM)
    lane_pad = ((0, 0), (0, 0), (0, 0), (0, LANES - C_NOPE))
    wukv = jnp.concatenate([jnp.pad(wukv[..., :C_NOPE], lane_pad).reshape(n_odd, C_KV_RANK, -1),
                            jnp.pad(wukv[..., C_NOPE:], lane_pad).reshape(n_odd, C_KV_RANK, -1)],
                           axis=2).astype(BF16)
    eye = jnp.eye(D_BLOCKS, dtype=F32)[None, None, :, None, :, None]
    dense = lambda w: (eye * w[:, :, :, :, None, :]).reshape(n_odd, 2, D_WIDTH, D_WIDTH)
    gate_w = jnp.concatenate([dense(o_lru_wa), dense(o_lru_wx)], axis=3).astype(BF16)
    gate_b = jnp.concatenate([o_lru_ba, o_lru_bx], axis=2)[:, :, None, :]
    e_w_pool_p = e_w_pool.astype(BF16)
    e_w_out_p = e_w_out.astype(BF16)
    o_w_out_p = o_w_out.astype(BF16)
    w1_p = w_mlp1.astype(BF16)
    w2_p = w_mlp2.astype(BF16)
    row = lambda g: g.reshape(g.shape[0], 1, g.shape[1])
    e_norm, o_norm, mlp_norm = row(e_norm_mix), row(o_norm_mix), row(norm_mlp)
    g_cq, g_ckv, conv_b, pool_scale = row(o_g_cq), row(o_g_ckv), row(o_conv_b), row(e_pool_scale)
    gf = final_norm.reshape(1, D)

    for layer in range(depth):
        li = layer // 2
        if layer % 2 == 0:
            q, k, v, u = _even_prep(x2, e_norm[li], e_w_in_p[li], cos_a, sin_a, seq, tm_prep)
            x2 = _even_mix(x2, q, k, v, u, e_sink[li], e_w_pool_p[li], pool_scale[li], e_w_out_p[li], seq, tq_even)
        else:
            q, k, vt, xr, xg = _odd_prep(x2, o_norm[li], o_w_in_p[li], g_cq[li], wuq[li], g_ckv[li], wukv[li],
                                         cos_c, sin_c, seq, tm_prep, tk_flash, q_scale_c)
            yc = _flash(q, k, vt, batch, seq, tq_flash, tk_flash)
            hf, hb = _lru(xr, o_conv_w[li], conv_b[li], gate_w[li, 0], gate_b[li, 0], gate_w[li, 1], gate_b[li, 1],
                          o_lru_lambda[li], batch, seq, tm_lru)
            x2 = _odd_out(x2, yc, hf, hb, xg, o_w_out_p[li], tm_out)
        x2 = _mlp(x2, mlp_norm[layer], w1_p[layer], w2_p[layer], gf, tm_mlp, tf_mlp, final_norm=(layer == depth - 1))
    return x2.reshape(batch, seq, D)
```

```python
import functools
import math

import jax
import jax.numpy as jnp
from jax import lax
from jax.experimental import pallas as pl
from jax.experimental.pallas import tpu as pltpu

F32 = jnp.float32
BF16 = jnp.bfloat16

EPS = 1e-6
ROPE_THETA = 10000.0
LANES = 128
SUBLANES = 8
MXU_DIM = 256
VMEM_LIMIT = 48 * 1024 * 1024

A_HEADS = 8
A_KV_HEADS = 2
A_HEAD_DIM = 64
A_WINDOW = 128
A_WIDTH = A_HEADS * A_HEAD_DIM
POOL_WINDOWS = (2, 4, 8, 16)
B_GROUP_DIM = 128
B_WIDTH = B_GROUP_DIM * len(POOL_WINDOWS)
POOL_HALO = 8
C_HEADS = 8
C_NOPE = 64
C_ROPE = 32
C_VDIM = 64
C_Q_RANK = 256
C_KV_RANK = 128
V_ROWS = 80
D_WIDTH = 512
D_BLOCKS = 8
D_BLOCK_DIM = 64
CONV_WIDTH = 4
LRU_C = 8.0

NEG_BIG = -1e30
LOG2E = math.log2(math.e)


def _params(*sem):
    return pltpu.CompilerParams(dimension_semantics=sem, vmem_limit_bytes=VMEM_LIMIT)


def _rms(x, g):
    return x * lax.rsqrt(jnp.mean(x * x, axis=-1, keepdims=True) + EPS) * g


def _rope_slab(xs, cos, sin, half):
    lane = lax.broadcasted_iota(jnp.int32, (1, LANES), 1)
    first = (lane % (2 * half)) < half
    swapped = jnp.where(first, pltpu.roll(xs, LANES - half, 1), pltpu.roll(xs, half, 1))
    return xs * cos + swapped * sin


def _even_prep_kernel(x_ref, g_ref, w_ref, cos_ref, sin_ref, q_ref, k_ref, v_ref, u_ref):
    h = _rms(x_ref[...], g_ref[...]).astype(BF16)
    z = jnp.dot(h, w_ref[...], preferred_element_type=F32)
    cos = cos_ref[...]
    sin = sin_ref[...]
    n_q = A_WIDTH // LANES
    for c in range(n_q):
        r = _rope_slab(z[:, c * LANES:(c + 1) * LANES], cos, sin, A_HEAD_DIM // 2)
        q_ref[:, c * LANES:(c + 1) * LANES] = (r * (A_HEAD_DIM ** -0.5 * LOG2E)).astype(BF16)
    for c in range(A_KV_HEADS):
        lo = A_WIDTH + c * LANES
        k_ref[:, c * LANES:(c + 1) * LANES] = _rope_slab(z[:, lo:lo + LANES], cos, sin, A_HEAD_DIM // 2).astype(BF16)
    lo = A_WIDTH + A_KV_HEADS * LANES
    v_ref[...] = z[:, lo:lo + A_KV_HEADS * LANES].astype(BF16)
    u_ref[...] = z[:, lo + A_KV_HEADS * LANES:]


def _even_prep(x2, g, w, cos, sin, seq, tm):
    T, D = x2.shape
    n_in = w.shape[1]
    kvw = A_KV_HEADS * LANES
    nt_seq = seq // tm
    tok = lambda i: (i, 0)
    tab = lambda i: (i % nt_seq, 0)
    fixed = lambda i: (0, 0)
    return pl.pallas_call(
        _even_prep_kernel,
        grid=(T // tm,),
        in_specs=[pl.BlockSpec((tm, D), tok), pl.BlockSpec((1, D), fixed), pl.BlockSpec((D, n_in), fixed),
                  pl.BlockSpec((tm, LANES), tab), pl.BlockSpec((tm, LANES), tab)],
        out_specs=[pl.BlockSpec((tm, A_WIDTH), tok), pl.BlockSpec((tm, kvw), tok),
                   pl.BlockSpec((tm, kvw), tok), pl.BlockSpec((tm, B_WIDTH), tok)],
        out_shape=[jax.ShapeDtypeStruct((T, A_WIDTH), BF16), jax.ShapeDtypeStruct((T, kvw), BF16),
                   jax.ShapeDtypeStruct((T, kvw), BF16), jax.ShapeDtypeStruct((T, B_WIDTH), F32)],
        compiler_params=_params("parallel"),
        name="even_prep",
    )(x2, g, w, cos, sin)


def _even_mix_kernel(sink_ref, x_ref, q_ref, kp_ref, kc_ref, kn_ref, vp_ref, vc_ref, vn_ref,
                     up_ref, uc_ref, un_ref, wpool_ref, pscale_ref, wout_ref, o_ref, uext_ref, *, seq, tq):
    i = pl.program_id(0)
    nt_seq = seq // tq
    pos0 = (i % nt_seq) * tq
    at_start = pos0 == 0
    at_end = pos0 + tq == seq
    nk = tq + 2 * A_WINDOW

    kd = jnp.concatenate([kp_ref[...], kc_ref[...], kn_ref[...]], axis=0)
    vd = jnp.concatenate([vp_ref[...], vc_ref[...], vn_ref[...]], axis=0)
    r = lax.broadcasted_iota(jnp.int32, (tq, nk), 0)
    c = lax.broadcasted_iota(jnp.int32, (tq, nk), 1)
    kpos = pos0 - A_WINDOW + c
    valid = (jnp.abs(c - A_WINDOW - r) <= A_WINDOW) & (kpos >= 0) & (kpos < seq)
    valid2 = jnp.concatenate([valid, valid], axis=0)
    lane = lax.broadcasted_iota(jnp.int32, (1, LANES), 1)
    low = lane < A_HEAD_DIM
    row2 = lax.broadcasted_iota(jnp.int32, (2 * tq, 1), 0)
    ya = []
    for j in range(A_HEADS // 2):
        g = (2 * j) // (A_HEADS // A_KV_HEADS)
        qp = q_ref[:, j * LANES:(j + 1) * LANES]
        zero = jnp.zeros_like(qp)
        lhs = jnp.concatenate([jnp.where(low, qp, zero), jnp.where(low, zero, qp)], axis=0)
        s = lax.dot_general(lhs, kd[:, g * LANES:(g + 1) * LANES], (((1,), (1,)), ((), ())),
                            preferred_element_type=F32)
        s = jnp.where(valid2, s, NEG_BIG)
        sk = jnp.where(row2 < tq, sink_ref[2 * j], sink_ref[2 * j + 1]) * LOG2E
        m = jnp.maximum(jnp.max(s, axis=1, keepdims=True), sk)
        p = jnp.exp2(s - m)
        den = jnp.sum(p, axis=1, keepdims=True) + jnp.exp2(sk - m)
        o = jnp.dot(p.astype(BF16), vd[:, g * LANES:(g + 1) * LANES], preferred_element_type=F32) / den
        ya.append(jnp.where(low, o[:tq], o[tq:]))

    uext_ref[0:POOL_HALO, :] = jnp.where(at_start, 0.0, up_ref[...])
    uext_ref[POOL_HALO:POOL_HALO + tq, :] = uc_ref[...]
    uext_ref[POOL_HALO + tq:, :] = jnp.where(at_end, 0.0, un_ref[...])
    t = pos0 + lax.broadcasted_iota(jnp.int32, (tq, 1), 0)
    yb = []
    for gi, w in enumerate(POOL_WINDOWS):
        half = w // 2
        cols = slice(gi * B_GROUP_DIM, (gi + 1) * B_GROUP_DIM)
        win = uext_ref[POOL_HALO - half:POOL_HALO - half + tq, cols]
        for off in range(-half + 1, half):
            win = win + uext_ref[POOL_HALO + off:POOL_HALO + off + tq, cols]
        cnt = (jnp.minimum(t + half, seq) - jnp.maximum(t - half, 0)).astype(F32)
        d = win / cnt - uc_ref[:, cols]
        y = jnp.dot(d.astype(BF16), wpool_ref[gi], preferred_element_type=F32)
        yb.append(y * pscale_ref[:, cols])

    ycat = jnp.concatenate(ya + yb, axis=1).astype(BF16)
    o_ref[...] = x_ref[...] + jnp.dot(ycat, wout_ref[...], preferred_element_type=F32)


def _even_mix(x2, q, k, v, u, sink, wpool, pscale, wout, seq, tq):
    T, D = x2.shape
    kvw = k.shape[1]
    nt_seq = seq // tq
    wb = tq // A_WINDOW
    hb = tq // POOL_HALO
    n_wb = T // A_WINDOW
    n_hb = T // POOL_HALO
    tok = lambda i: (i, 0)
    prev_w = lambda i: (jnp.maximum(i * wb - 1, 0), 0)
    next_w = lambda i: (jnp.minimum((i + 1) * wb, n_wb - 1), 0)
    prev_h = lambda i: (jnp.maximum(i * hb - 1, 0), 0)
    next_h = lambda i: (jnp.minimum((i + 1) * hb, n_hb - 1), 0)
    fixed2 = lambda i: (0, 0)
    fixed3 = lambda i: (0, 0, 0)
    return pl.pallas_call(
        functools.partial(_even_mix_kernel, seq=seq, tq=tq),
        grid=(T // tq,),
        in_specs=[pl.BlockSpec(memory_space=pltpu.SMEM),
                  pl.BlockSpec((tq, D), tok),
                  pl.BlockSpec((tq, A_WIDTH), tok),
                  pl.BlockSpec((A_WINDOW, kvw), prev_w), pl.BlockSpec((tq, kvw), tok), pl.BlockSpec((A_WINDOW, kvw), next_w),
                  pl.BlockSpec((A_WINDOW, kvw), prev_w), pl.BlockSpec((tq, kvw), tok), pl.BlockSpec((A_WINDOW, kvw), next_w),
                  pl.BlockSpec((POOL_HALO, B_WIDTH), prev_h), pl.BlockSpec((tq, B_WIDTH), tok), pl.BlockSpec((POOL_HALO, B_WIDTH), next_h),
                  pl.BlockSpec(wpool.shape, fixed3), pl.BlockSpec((1, B_WIDTH), fixed2), pl.BlockSpec((D, D), fixed2)],
        out_specs=pl.BlockSpec((tq, D), tok),
        out_shape=jax.ShapeDtypeStruct((T, D), F32),
        scratch_shapes=[pltpu.VMEM((tq + 2 * POOL_HALO, B_WIDTH), F32)],
        compiler_params=_params("parallel"),
        name="even_mix",
    )(sink, x2, q, k, k, k, v, v, v, u, u, u, wpool, pscale, wout)


def _mlp_kernel(x_ref, g_ref, w1_ref, w2_ref, gf_ref, o_ref, h_ref, *, final_norm):
    f = pl.program_id(1)

    @pl.when(f == 0)
    def _():
        x = x_ref[...]
        h_ref[...] = _rms(x, g_ref[...]).astype(BF16)
        o_ref[...] = x

    u = jnp.maximum(jnp.dot(h_ref[...], w1_ref[...], preferred_element_type=F32), 0.0)
    o_ref[...] += jnp.dot((u * u).astype(BF16), w2_ref[...], preferred_element_type=F32)

    if final_norm:
        @pl.when(f == pl.num_programs(1) - 1)
        def _():
            o_ref[...] = _rms(o_ref[...], gf_ref[...])


def _mlp(x2, g, w1, w2, gf, tm, tf, final_norm):
    T, D = x2.shape
    F = w1.shape[1]
    return pl.pallas_call(
        functools.partial(_mlp_kernel, final_norm=final_norm),
        grid=(T // tm, F // tf),
        in_specs=[pl.BlockSpec((tm, D), lambda i, f: (i, 0)), pl.BlockSpec((1, D), lambda i, f: (0, 0)),
                  pl.BlockSpec((D, tf), lambda i, f: (0, f)), pl.BlockSpec((tf, D), lambda i, f: (f, 0)),
                  pl.BlockSpec((1, D), lambda i, f: (0, 0))],
        out_specs=pl.BlockSpec((tm, D), lambda i, f: (i, 0)),
        out_shape=jax.ShapeDtypeStruct((T, D), F32),
        scratch_shapes=[pltpu.VMEM((tm, D), BF16)],
        compiler_params=_params("parallel", "arbitrary"),
        name="mlp",
    )(x2, g, w1, w2, gf)


def _odd_prep_kernel(x_ref, g_ref, w_ref, gq_ref, wuq_ref, gkv_ref, wukv_ref, cos_ref, sin_ref,
                     q_ref, k_ref, vt_ref, xr_ref, xg_ref, *, q_scale):
    h = _rms(x_ref[...], g_ref[...]).astype(BF16)
    z = jnp.dot(h, w_ref[...], preferred_element_type=F32)
    cos = cos_ref[...]
    sin = sin_ref[...]
    o1 = C_Q_RANK
    o2 = o1 + C_KV_RANK
    o3 = o2 + LANES
    cq = _rms(z[:, :o1], gq_ref[...]).astype(BF16)
    ckv = _rms(z[:, o1:o2], gkv_ref[...]).astype(BF16)
    kr = _rope_slab(z[:, o2:o3], cos, sin, C_ROPE // 2)
    q = jnp.dot(cq, wuq_ref[...], preferred_element_type=F32)
    kv = jnp.dot(ckv, wukv_ref[...], preferred_element_type=F32)
    lane = lax.broadcasted_iota(jnp.int32, (1, LANES), 1)
    ones_col = (lane == C_VDIM).astype(F32)
    for hd in range(C_HEADS):
        qs = _rope_slab(q[:, hd * LANES:(hd + 1) * LANES], cos, sin, C_ROPE // 2)
        q_ref[hd] = (qs * q_scale).astype(BF16)
        k_ref[hd] = (kv[:, hd * LANES:(hd + 1) * LANES] + kr).astype(BF16)
        vo = (C_HEADS + hd) * LANES
        vt_ref[hd] = (kv[:, vo:vo + LANES] + ones_col).T[:V_ROWS].astype(BF16)
    xr_ref[...] = z[:, o3:o3 + D_WIDTH]
    xg_ref[...] = z[:, o3 + D_WIDTH:]


def _odd_prep(x2, g, w, gq, wuq, gkv, wukv, cos, sin, seq, tm, tk, q_scale):
    T, D = x2.shape
    nt_seq = seq // tm
    per_chunk = tk // tm
    tok = lambda i: (i, 0)
    tab = lambda i: (i % nt_seq, 0)
    fixed = lambda i: (0, 0)
    head = lambda i: (0, i, 0)
    hshape = jax.ShapeDtypeStruct((C_HEADS, T, LANES), BF16)
    return pl.pallas_call(
        functools.partial(_odd_prep_kernel, q_scale=q_scale),
        grid=(T // tm,),
        in_specs=[pl.BlockSpec((tm, D), tok), pl.BlockSpec((1, D), fixed), pl.BlockSpec(w.shape, fixed),
                  pl.BlockSpec((1, C_Q_RANK), fixed), pl.BlockSpec(wuq.shape, fixed),
                  pl.BlockSpec((1, C_KV_RANK), fixed), pl.BlockSpec(wukv.shape, fixed),
                  pl.BlockSpec((tm, LANES), tab), pl.BlockSpec((tm, LANES), tab)],
        out_specs=[pl.BlockSpec((C_HEADS, tm, LANES), head), pl.BlockSpec((C_HEADS, tm, LANES), head),
                   pl.BlockSpec((C_HEADS, None, V_ROWS, tm), lambda i: (0, i // per_chunk, 0, i % per_chunk)),
                   pl.BlockSpec((tm, D_WIDTH), tok), pl.BlockSpec((tm, D_WIDTH), tok)],
        out_shape=[hshape, hshape, jax.ShapeDtypeStruct((C_HEADS, T // tk, V_ROWS, tk), BF16),
                   jax.ShapeDtypeStruct((T, D_WIDTH), F32), jax.ShapeDtypeStruct((T, D_WIDTH), F32)],
        compiler_params=_params("parallel"),
        name="odd_prep",
    )(x2, g, w, gq, wuq, gkv, wukv, cos, sin)


def _flash_kernel(q_ref, k_ref, vt_ref, o_ref, s_ref, p_ref, acc_ref, *, tq, tk):
    seq = k_ref.shape[0]
    n_chunks = seq // tk
    n_steps = (seq // tq) * n_chunks

    def scores(g):
        tile = g // n_chunks
        q = q_ref[pl.ds(pl.multiple_of(tile * tq, tq), tq), :]
        kc = k_ref[pl.ds(pl.multiple_of((g % n_chunks) * tk, tk), tk), :]
        st = lax.dot_general(kc, q, (((1,), (1,)), ((), ())), preferred_element_type=F32)
        s_ref[...] = st
        return jnp.max(st, axis=0, keepdims=True)

    def values(g, alpha):
        acc_ref[...] = alpha * acc_ref[...] + jnp.dot(vt_ref[g % n_chunks], p_ref[...],
                                                       preferred_element_type=F32)

    def finish_tile(tile):
        acc = acc_ref[...]
        out = acc[:C_VDIM] / acc[C_VDIM:C_VDIM + 1, :]
        out = jnp.concatenate([out, jnp.zeros((LANES - C_VDIM, tq), F32)], axis=0)
        o_ref[pl.ds(pl.multiple_of(tile * tq, tq), tq), :] = out.T.astype(BF16)
        acc_ref[...] = jnp.zeros_like(acc_ref)

    p_ref[...] = jnp.zeros_like(p_ref)
    acc_ref[...] = jnp.zeros_like(acc_ref)
    mx0 = scores(0)

    def body(g, carry):
        m, mx, alpha = carry
        first = g % n_chunks == 0
        values(jnp.maximum(g - 1, 0), alpha)
        m_old = jnp.where(first, NEG_BIG, m)
        m_new = jnp.maximum(m_old, mx)
        p_ref[...] = jnp.exp2(s_ref[...] - m_new).astype(BF16)
        mx_next = scores(jnp.minimum(g + 1, n_steps - 1))

        @pl.when(jnp.logical_and(first, g > 0))
        def _():
            finish_tile(g // n_chunks - 1)

        return m_new, mx_next, jnp.exp2(m_old - m_new)

    m0 = jnp.full((1, tq), NEG_BIG, F32)
    _, _, alpha = lax.fori_loop(0, n_steps, body, (m0, mx0, jnp.ones((1, tq), F32)))
    values(n_steps - 1, alpha)
    finish_tile(seq // tq - 1)


def _flash(q, k, vt, batch, seq, tq, tk):
    H, T, _ = q.shape
    n_chunks = seq // tk
    per_head = lambda b, h: (h, b, 0)
    return pl.pallas_call(
        functools.partial(_flash_kernel, tq=tq, tk=tk),
        grid=(batch, H),
        in_specs=[pl.BlockSpec((None, seq, LANES), per_head),
                  pl.BlockSpec((None, seq, LANES), per_head),
                  pl.BlockSpec((None, n_chunks, V_ROWS, tk), lambda b, h: (h, b, 0, 0))],
        out_specs=pl.BlockSpec((None, seq, LANES), per_head),
        out_shape=jax.ShapeDtypeStruct((H, T, LANES), BF16),
        scratch_shapes=[pltpu.VMEM((tk, tq), F32), pltpu.VMEM((tk, tq), BF16), pltpu.VMEM((V_ROWS, tq), F32)],
        compiler_params=_params("parallel", "parallel"),
        name="mla_flash",
    )(q, k, vt)


def _scan_tile(a, b, carry, reverse):
    rows, width = a.shape
    groups = rows // SUBLANES
    a = a.reshape(groups, SUBLANES, width)
    b = b.reshape(groups, SUBLANES, width)
    sub = lax.broadcasted_iota(jnp.int32, (1, SUBLANES, 1), 1)
    k = 1
    while k < SUBLANES:
        shift = (SUBLANES - k) if reverse else k
        ok = (sub < SUBLANES - k) if reverse else (sub >= k)
        a_s = pltpu.roll(a, shift, 1)
        b_s = pltpu.roll(b, shift, 1)
        b = jnp.where(ok, a * b_s + b, b)
        a = jnp.where(ok, a * a_s, a)
        k *= 2
    order = range(groups - 1, -1, -1) if reverse else range(groups)
    edge = 0 if reverse else SUBLANES - 1
    out = [None] * groups
    for gi in order:
        hg = a[gi] * carry + b[gi]
        out[gi] = hg
        carry = hg[edge:edge + 1]
    return jnp.concatenate(out, axis=0), carry


def _lru_gates(xp_ref, xc_ref, xn_ref, lo_edge, hi_edge, cw_ref, cb_ref, wg_ref, bg_ref, lam_row, ext_ref):
    tm = xc_ref.shape[0]
    ext_ref[0:SUBLANES, :] = jnp.where(lo_edge, 0.0, xp_ref[...])
    ext_ref[SUBLANES:SUBLANES + tm, :] = xc_ref[...]
    ext_ref[SUBLANES + tm:, :] = jnp.where(hi_edge, 0.0, xn_ref[...])
    left = CONV_WIDTH // 2
    xc = cb_ref[...] + cw_ref[0:1, :] * ext_ref[SUBLANES - left:SUBLANES - left + tm, :]
    for j in range(1, CONV_WIDTH):
        xc = xc + cw_ref[j:j + 1, :] * ext_ref[SUBLANES - left + j:SUBLANES - left + j + tm, :]
    gates = jnp.dot(xc.astype(BF16), wg_ref[...], preferred_element_type=F32) + bg_ref[...]
    gates = 1.0 / (1.0 + jnp.exp(-gates))
    r = gates[:, :D_WIDTH]
    ig = gates[:, D_WIDTH:]
    neg = -lam_row
    softplus = jnp.maximum(neg, 0.0) + jnp.log1p(jnp.exp(-jnp.abs(neg)))
    log_a = (-LRU_C) * r * softplus
    a = jnp.exp(log_a)
    th = jnp.tanh(log_a)
    b = jnp.sqrt(-2.0 * th / (1.0 - th)) * (ig * xc)
    return a, b


def _lru_kernel(fp_ref, fc_ref, fn_ref, bp_ref, bc_ref, bn_ref, cw_ref, cb_ref, wgf_ref, bgf_ref,
                wgb_ref, bgb_ref, lam_ref, hf_ref, hb_ref, ext_ref, carry_ref):
    i = pl.program_id(1)
    last = pl.num_programs(1) - 1

    @pl.when(i == 0)
    def _():
        carry_ref[...] = jnp.zeros_like(carry_ref)

    a, b = _lru_gates(fp_ref, fc_ref, fn_ref, i == 0, i == last, cw_ref, cb_ref, wgf_ref, bgf_ref,
                      lam_ref[0:1, :], ext_ref)
    h, cf = _scan_tile(a, b, carry_ref[0:1, :], reverse=False)
    hf_ref[...] = h
    carry_ref[0:1, :] = cf

    a, b = _lru_gates(bp_ref, bc_ref, bn_ref, i == last, i == 0, cw_ref, cb_ref, wgb_ref, bgb_ref,
                      lam_ref[1:2, :], ext_ref)
    h, cb = _scan_tile(a, b, carry_ref[1:2, :], reverse=True)
    hb_ref[...] = h
    carry_ref[1:2, :] = cb


def _lru(xr, cw, cb, wgf, bgf, wgb, bgb, lam, batch, seq, tm):
    T, W = xr.shape
    nt = seq // tm
    hb = tm // SUBLANES
    n_hb = T // SUBLANES
    f_tok = lambda b, i: (b * nt + i, 0)
    b_tok = lambda b, i: (b * nt + nt - 1 - i, 0)
    f_prev = lambda b, i: (jnp.maximum((b * nt + i) * hb - 1, 0), 0)
    f_next = lambda b, i: (jnp.minimum((b * nt + i + 1) * hb, n_hb - 1), 0)
    b_prev = lambda b, i: (jnp.maximum((b * nt + nt - 1 - i) * hb - 1, 0), 0)
    b_next = lambda b, i: (jnp.minimum((b * nt + nt - i) * hb, n_hb - 1), 0)
    fixed = lambda b, i: (0, 0)
    halo = (SUBLANES, W)
    return pl.pallas_call(
        _lru_kernel,
        grid=(batch, nt),
        in_specs=[pl.BlockSpec(halo, f_prev), pl.BlockSpec((tm, W), f_tok), pl.BlockSpec(halo, f_next),
                  pl.BlockSpec(halo, b_prev), pl.BlockSpec((tm, W), b_tok), pl.BlockSpec(halo, b_next),
                  pl.BlockSpec(cw.shape, fixed), pl.BlockSpec(cb.shape, fixed),
                  pl.BlockSpec(wgf.shape, fixed), pl.BlockSpec(bgf.shape, fixed),
                  pl.BlockSpec(wgb.shape, fixed), pl.BlockSpec(bgb.shape, fixed),
                  pl.BlockSpec(lam.shape, fixed)],
        out_specs=[pl.BlockSpec((tm, W), f_tok), pl.BlockSpec((tm, W), b_tok)],
        out_shape=[jax.ShapeDtypeStruct((T, W), F32), jax.ShapeDtypeStruct((T, W), F32)],
        scratch_shapes=[pltpu.VMEM((tm + 2 * SUBLANES, W), F32), pltpu.VMEM((2, W), F32)],
        compiler_params=_params("arbitrary", "arbitrary"),
        name="rglru",
    )(xr, xr, xr, xr, xr, xr, cw, cb, wgf, bgf, wgb, bgb, lam)


def _gelu_tanh(x):
    return 0.5 * x * (1.0 + jnp.tanh(math.sqrt(2.0 / math.pi) * (x + 0.044715 * (x * x * x))))


def _odd_out_kernel(x_ref, yc_ref, hf_ref, hb_ref, xg_ref, w_ref, o_ref):
    lane = lax.broadcasted_iota(jnp.int32, (1, LANES), 1)
    low = lane < C_VDIM
    parts = []
    for hd in range(0, C_HEADS, 2):
        even = yc_ref[hd].astype(F32)
        odd = pltpu.roll(yc_ref[hd + 1].astype(F32), C_VDIM, 1)
        parts.append(jnp.where(low, even, odd).astype(BF16))
    parts.append(((hf_ref[...] + hb_ref[...]) * _gelu_tanh(xg_ref[...])).astype(BF16))
    ycat = jnp.concatenate(parts, axis=1)
    o_ref[...] = x_ref[...] + jnp.dot(ycat, w_ref[...], preferred_element_type=F32)


def _odd_out(x2, yc, hf, hb, xg, w, tm):
    T, D = x2.shape
    tok = lambda i: (i, 0)
    return pl.pallas_call(
        _odd_out_kernel,
        grid=(T // tm,),
        in_specs=[pl.BlockSpec((tm, D), tok), pl.BlockSpec((C_HEADS, tm, LANES), lambda i: (0, i, 0)),
                  pl.BlockSpec((tm, D_WIDTH), tok), pl.BlockSpec((tm, D_WIDTH), tok), pl.BlockSpec((tm, D_WIDTH), tok),
                  pl.BlockSpec(w.shape, lambda i: (0, 0))],
        out_specs=pl.BlockSpec((tm, D), tok),
        out_shape=jax.ShapeDtypeStruct((T, D), F32),
        compiler_params=_params("parallel"),
        name="odd_out",
    )(x2, yc, hf, hb, xg, w)


def _rope_tables(seq, half, lane_lo, lane_hi):
    inv = ROPE_THETA ** (-jnp.arange(half, dtype=F32) / half)
    ang = jnp.arange(seq, dtype=F32)[:, None] * inv[None, :]
    lane = jnp.arange(LANES)
    idx = lane % half
    sign = jnp.where((lane % (2 * half)) < half, -1.0, 1.0).astype(F32)
    active = (lane >= lane_lo) & (lane < lane_hi)
    cos = jnp.where(active[None, :], jnp.cos(ang)[:, idx], 1.0)
    sin = jnp.where(active[None, :], jnp.sin(ang)[:, idx] * sign[None, :], 0.0)
    return cos, sin


def _tile_of(n, pref):
    t = min(n, pref)
    assert n % t == 0
    return t


def kernel(x, e_norm_mix, e_w_in, e_sink, e_w_pool, e_pool_scale, e_w_out, o_norm_mix, o_w_in, o_g_cq, o_w_uq,
           o_g_ckv, o_w_ukv, o_conv_w, o_conv_b, o_lru_wa, o_lru_ba, o_lru_wx, o_lru_bx, o_lru_lambda, o_w_out,
           norm_mlp, w_mlp1, w_mlp2, final_norm):
    batch, seq, D = x.shape
    T = batch * seq
    depth = norm_mlp.shape[0]
    x2 = x.reshape(T, D)

    tm_prep = _tile_of(seq, 512)
    tq_even = _tile_of(seq, 256)
    tm_mlp = _tile_of(T, 1024)
    tf_mlp = _tile_of(w_mlp1.shape[2], 1024)
    tq_flash = _tile_of(seq, 512)
    tk_flash = _tile_of(seq, 4096)
    tm_lru = _tile_of(seq, 256)
    tm_out = _tile_of(seq, 512)

    cos_a, sin_a = _rope_tables(seq, A_HEAD_DIM // 2, 0, LANES)
    cos_c, sin_c = _rope_tables(seq, C_ROPE // 2, C_NOPE, C_NOPE + C_ROPE)
    q_scale_c = (C_NOPE + C_ROPE) ** -0.5 * LOG2E

    hd = A_HEAD_DIM
    kv_heads = [e_w_in[:, :, A_WIDTH + j * hd:A_WIDTH + (j + 1) * hd] for j in range(2 * A_KV_HEADS)]
    e_w_in_p = jnp.concatenate([e_w_in[:, :, :A_WIDTH]] + [w for w in kv_heads for _ in range(2)]
                               + [e_w_in[:, :, A_WIDTH + 2 * A_KV_HEADS * hd:]],
                               axis=2).astype(BF16)
    i2 = C_Q_RANK + C_KV_RANK
    i3 = i2 + C_ROPE
    kr_slab = jnp.pad(o_w_in[:, :, i2:i3], ((0, 0), (0, 0), (C_NOPE, LANES - C_NOPE - C_ROPE)))
    o_w_in_p = jnp.concatenate([o_w_in[:, :, :i2], kr_slab, o_w_in[:, :, i3:]], axis=2).astype(BF16)
    n_odd = o_w_in.shape[0]
    wuq = o_w_uq.reshape(n_odd, C_Q_RANK, C_HEADS, C_NOPE + C_ROPE)
    wuq = jnp.pad(wuq, ((0, 0), (0, 0), (0, 0), (0, LANES - C_NOPE - C_ROPE)))
    wuq = wuq.reshape(n_odd, C_Q_RANK, C_HEADS * LANES).astype(BF16)
    wukv = o_w_ukv.reshape(n_odd, C_KV_RANK, C_HEADS, C_NOPE + C_VDIM)
    lane_pad = ((0, 0), (0, 0), (0, 0), (0, LANES - C_NOPE))
    wukv = jnp.concatenate([jnp.pad(wukv[..., :C_NOPE], lane_pad).reshape(n_odd, C_KV_RANK, -1),
                            jnp.pad(wukv[..., C_NOPE:], lane_pad).reshape(n_odd, C_KV_RANK, -1)],
                           axis=2).astype(BF16)
    eye = jnp.eye(D_BLOCKS, dtype=F32)[None, None, :, None, :, None]
    dense = lambda w: (eye * w[:, :, :, :, None, :]).reshape(n_odd, 2, D_WIDTH, D_WIDTH)
    gate_w = jnp.concatenate([dense(o_lru_wa), dense(o_lru_wx)], axis=3).astype(BF16)
    gate_b = jnp.concatenate([o_lru_ba, o_lru_bx], axis=2)[:, :, None, :]
    e_w_pool_p = e_w_pool.astype(BF16)
    e_w_out_p = e_w_out.astype(BF16)
    o_w_out_p = o_w_out.astype(BF16)
    w1_p = w_mlp1.astype(BF16)
    w2_p = w_mlp2.astype(BF16)
    row = lambda g: g.reshape(g.shape[0], 1, g.shape[1])
    e_norm, o_norm, mlp_norm = row(e_norm_mix), row(o_norm_mix), row(norm_mlp)
    g_cq, g_ckv, conv_b, pool_scale = row(o_g_cq), row(o_g_ckv), row(o_conv_b), row(e_pool_scale)
    gf = final_norm.reshape(1, D)

    for layer in range(depth):
        li = layer // 2
        if layer % 2 == 0:
            q, k, v, u = _even_prep(x2, e_norm[li], e_w_in_p[li], cos_a, sin_a, seq, tm_prep)
            x2 = _even_mix(x2, q, k, v, u, e_sink[li], e_w_pool_p[li], pool_scale[li], e_w_out_p[li], seq, tq_even)
        else:
            q, k, vt, xr, xg = _odd_prep(x2, o_norm[li], o_w_in_p[li], g_cq[li], wuq[li], g_ckv[li], wukv[li],
                                         cos_c, sin_c, seq, tm_prep, tk_flash, q_scale_c)
            yc = _flash(q, k, vt, batch, seq, tq_flash, tk_flash)
            hf, hb = _lru(xr, o_conv_w[li], conv_b[li], gate_w[li, 0], gate_b[li, 0], gate_w[li, 1], gate_b[li, 1],
                          o_lru_lambda[li], batch, seq, tm_lru)
            x2 = _odd_out(x2, yc, hf, hb, xg, o_w_out_p[li], tm_out)
        x2 = _mlp(x2, mlp_norm[layer], w1_p[layer], w2_p[layer], gf, tm_mlp, tf_mlp, final_norm=(layer == depth - 1))
    return x2.reshape(batch, seq, D)
```

```python
import functools
import math

import jax
import jax.numpy as jnp
from jax import lax
from jax.experimental import pallas as pl
from jax.experimental.pallas import tpu as pltpu

F32 = jnp.float32
BF16 = jnp.bfloat16

EPS = 1e-6
ROPE_THETA = 10000.0
LANES = 128
SUBLANES = 8
MXU_DIM = 256
VMEM_LIMIT = 48 * 1024 * 1024

A_HEADS = 8
A_KV_HEADS = 2
A_HEAD_DIM = 64
A_WINDOW = 128
A_WIDTH = A_HEADS * A_HEAD_DIM
POOL_WINDOWS = (2, 4, 8, 16)
B_GROUP_DIM = 128
B_WIDTH = B_GROUP_DIM * len(POOL_WINDOWS)
POOL_HALO = 8
C_HEADS = 8
C_NOPE = 64
C_ROPE = 32
C_VDIM = 64
C_Q_RANK = 256
C_KV_RANK = 128
V_ROWS = 80
D_WIDTH = 512
D_BLOCKS = 8
D_BLOCK_DIM = 64
CONV_WIDTH = 4
LRU_C = 8.0

NEG_BIG = -1e30
LOG2E = math.log2(math.e)


def _params(*sem):
    return pltpu.CompilerParams(dimension_semantics=sem, vmem_limit_bytes=VMEM_LIMIT)


def _rms(x, g):
    return x * lax.rsqrt(jnp.mean(x * x, axis=-1, keepdims=True) + EPS) * g


def _rope_slab(xs, cos, sin, half):
    lane = lax.broadcasted_iota(jnp.int32, (1, LANES), 1)
    first = (lane % (2 * half)) < half
    swapped = jnp.where(first, pltpu.roll(xs, LANES - half, 1), pltpu.roll(xs, half, 1))
    return xs * cos + swapped * sin


def _even_prep_kernel(x_ref, g_ref, w_ref, cos_ref, sin_ref, q_ref, k_ref, v_ref, u_ref):
    h = _rms(x_ref[...], g_ref[...]).astype(BF16)
    z = jnp.dot(h, w_ref[...], preferred_element_type=F32)
    cos = cos_ref[...]
    sin = sin_ref[...]
    n_q = A_WIDTH // LANES
    for c in range(n_q):
        r = _rope_slab(z[:, c * LANES:(c + 1) * LANES], cos, sin, A_HEAD_DIM // 2)
        q_ref[:, c * LANES:(c + 1) * LANES] = (r * (A_HEAD_DIM ** -0.5 * LOG2E)).astype(BF16)
    for c in range(A_KV_HEADS):
        lo = A_WIDTH + c * LANES
        k_ref[:, c * LANES:(c + 1) * LANES] = _rope_slab(z[:, lo:lo + LANES], cos, sin, A_HEAD_DIM // 2).astype(BF16)
    lo = A_WIDTH + A_KV_HEADS * LANES
    v_ref[...] = z[:, lo:lo + A_KV_HEADS * LANES].astype(BF16)
    u_ref[...] = z[:, lo + A_KV_HEADS * LANES:]


def _even_prep(x2, g, w, cos, sin, seq, tm):
    T, D = x2.shape
    n_in = w.shape[1]
    kvw = A_KV_HEADS * LANES
    nt_seq = seq // tm
    tok = lambda i: (i, 0)
    tab = lambda i: (i % nt_seq, 0)
    fixed = lambda i: (0, 0)
    return pl.pallas_call(
        _even_prep_kernel,
        grid=(T // tm,),
        in_specs=[pl.BlockSpec((tm, D), tok), pl.BlockSpec((1, D), fixed), pl.BlockSpec((D, n_in), fixed),
                  pl.BlockSpec((tm, LANES), tab), pl.BlockSpec((tm, LANES), tab)],
        out_specs=[pl.BlockSpec((tm, A_WIDTH), tok), pl.BlockSpec((tm, kvw), tok),
                   pl.BlockSpec((tm, kvw), tok), pl.BlockSpec((tm, B_WIDTH), tok)],
        out_shape=[jax.ShapeDtypeStruct((T, A_WIDTH), BF16), jax.ShapeDtypeStruct((T, kvw), BF16),
                   jax.ShapeDtypeStruct((T, kvw), BF16), jax.ShapeDtypeStruct((T, B_WIDTH), F32)],
        compiler_params=_params("parallel"),
        name="even_prep",
    )(x2, g, w, cos, sin)


def _even_mix_kernel(sink_ref, x_ref, q_ref, kp_ref, kc_ref, kn_ref, vp_ref, vc_ref, vn_ref,
                     up_ref, uc_ref, un_ref, wpool_ref, pscale_ref, wout_ref, o_ref, uext_ref, *, seq, tq):
    i = pl.program_id(0)
    nt_seq = seq // tq
    pos0 = (i % nt_seq) * tq
    at_start = pos0 == 0
    at_end = pos0 + tq == seq
    nk = tq + 2 * A_WINDOW

    kd = jnp.concatenate([kp_ref[...], kc_ref[...], kn_ref[...]], axis=0)
    vd = jnp.concatenate([vp_ref[...], vc_ref[...], vn_ref[...]], axis=0)
    r = lax.broadcasted_iota(jnp.int32, (tq, nk), 0)
    c = lax.broadcasted_iota(jnp.int32, (tq, nk), 1)
    kpos = pos0 - A_WINDOW + c
    valid = (jnp.abs(c - A_WINDOW - r) <= A_WINDOW) & (kpos >= 0) & (kpos < seq)
    valid2 = jnp.concatenate([valid, valid], axis=0)
    lane = lax.broadcasted_iota(jnp.int32, (1, LANES), 1)
    low = lane < A_HEAD_DIM
    row2 = lax.broadcasted_iota(jnp.int32, (2 * tq, 1), 0)
    ya = []
    for j in range(A_HEADS // 2):
        g = (2 * j) // (A_HEADS // A_KV_HEADS)
        qp = q_ref[:, j * LANES:(j + 1) * LANES]
        zero = jnp.zeros_like(qp)
        lhs = jnp.concatenate([jnp.where(low, qp, zero), jnp.where(low, zero, qp)], axis=0)
        s = lax.dot_general(lhs, kd[:, g * LANES:(g + 1) * LANES], (((1,), (1,)), ((), ())),
                            preferred_element_type=F32)
        s = jnp.where(valid2, s, NEG_BIG)
        sk = jnp.where(row2 < tq, sink_ref[2 * j], sink_ref[2 * j + 1]) * LOG2E
        m = jnp.maximum(jnp.max(s, axis=1, keepdims=True), sk)
        p = jnp.exp2(s - m)
        den = jnp.sum(p, axis=1, keepdims=True) + jnp.exp2(sk - m)
        o = jnp.dot(p.astype(BF16), vd[:, g * LANES:(g + 1) * LANES], preferred_element_type=F32) / den
        ya.append(jnp.where(low, o[:tq], o[tq:]))

    uext_ref[0:POOL_HALO, :] = jnp.where(at_start, 0.0, up_ref[...])
    uext_ref[POOL_HALO:POOL_HALO + tq, :] = uc_ref[...]
    uext_ref[POOL_HALO + tq:, :] = jnp.where(at_end, 0.0, un_ref[...])
    t = pos0 + lax.broadcasted_iota(jnp.int32, (tq, 1), 0)
    yb = []
    for gi, w in enumerate(POOL_WINDOWS):
        half = w // 2
        cols = slice(gi * B_GROUP_DIM, (gi + 1) * B_GROUP_DIM)
        win = uext_ref[POOL_HALO - half:POOL_HALO - half + tq, cols]
        for off in range(-half + 1, half):
            win = win + uext_ref[POOL_HALO + off:POOL_HALO + off + tq, cols]
        cnt = (jnp.minimum(t + half, seq) - jnp.maximum(t - half, 0)).astype(F32)
        d = win / cnt - uc_ref[:, cols]
        y = jnp.dot(d.astype(BF16), wpool_ref[gi], preferred_element_type=F32)
        yb.append(y * pscale_ref[:, cols])

    ycat = jnp.concatenate(ya + yb, axis=1).astype(BF16)
    o_ref[...] = x_ref[...] + jnp.dot(ycat, wout_ref[...], preferred_element_type=F32)


def _even_mix(x2, q, k, v, u, sink, wpool, pscale, wout, seq, tq):
    T, D = x2.shape
    kvw = k.shape[1]
    nt_seq = seq // tq
    wb = tq // A_WINDOW
    hb = tq // POOL_HALO
    n_wb = T // A_WINDOW
    n_hb = T // POOL_HALO
    tok = lambda i: (i, 0)
    prev_w = lambda i: (jnp.maximum(i * wb - 1, 0), 0)
    next_w = lambda i: (jnp.minimum((i + 1) * wb, n_wb - 1), 0)
    prev_h = lambda i: (jnp.maximum(i * hb - 1, 0), 0)
    next_h = lambda i: (jnp.minimum((i + 1) * hb, n_hb - 1), 0)
    fixed2 = lambda i: (0, 0)
    fixed3 = lambda i: (0, 0, 0)
    return pl.pallas_call(
        functools.partial(_even_mix_kernel, seq=seq, tq=tq),
        grid=(T // tq,),
        in_specs=[pl.BlockSpec(memory_space=pltpu.SMEM),
                  pl.BlockSpec((tq, D), tok),
                  pl.BlockSpec((tq, A_WIDTH), tok),
                  pl.BlockSpec((A_WINDOW, kvw), prev_w), pl.BlockSpec((tq, kvw), tok), pl.BlockSpec((A_WINDOW, kvw), next_w),
                  pl.BlockSpec((A_WINDOW, kvw), prev_w), pl.BlockSpec((tq, kvw), tok), pl.BlockSpec((A_WINDOW, kvw), next_w),
                  pl.BlockSpec((POOL_HALO, B_WIDTH), prev_h), pl.BlockSpec((tq, B_WIDTH), tok), pl.BlockSpec((POOL_HALO, B_WIDTH), next_h),
                  pl.BlockSpec(wpool.shape, fixed3), pl.BlockSpec((1, B_WIDTH), fixed2), pl.BlockSpec((D, D), fixed2)],
        out_specs=pl.BlockSpec((tq, D), tok),
        out_shape=jax.ShapeDtypeStruct((T, D), F32),
        scratch_shapes=[pltpu.VMEM((tq + 2 * POOL_HALO, B_WIDTH), F32)],
        compiler_params=_params("parallel"),
        name="even_mix",
    )(sink, x2, q, k, k, k, v, v, v, u, u, u, wpool, pscale, wout)


def _mlp_kernel(x_ref, g_ref, w1_ref, w2_ref, gf_ref, o_ref, h_ref, *, final_norm):
    f = pl.program_id(1)

    @pl.when(f == 0)
    def _():
        x = x_ref[...]
        h_ref[...] = _rms(x, g_ref[...]).astype(BF16)
        o_ref[...] = x

    u = jnp.maximum(jnp.dot(h_ref[...], w1_ref[...], preferred_element_type=F32), 0.0)
    o_ref[...] += jnp.dot((u * u).astype(BF16), w2_ref[...], preferred_element_type=F32)

    if final_norm:
        @pl.when(f == pl.num_programs(1) - 1)
        def _():
            o_ref[...] = _rms(o_ref[...], gf_ref[...])


def _mlp(x2, g, w1, w2, gf, tm, tf, final_norm):
    T, D = x2.shape
    F = w1.shape[1]
    return pl.pallas_call(
        functools.partial(_mlp_kernel, final_norm=final_norm),
        grid=(T // tm, F // tf),
        in_specs=[pl.BlockSpec((tm, D), lambda i, f: (i, 0)), pl.BlockSpec((1, D), lambda i, f: (0, 0)),
                  pl.BlockSpec((D, tf), lambda i, f: (0, f)), pl.BlockSpec((tf, D), lambda i, f: (f, 0)),
                  pl.BlockSpec((1, D), lambda i, f: (0, 0))],
        out_specs=pl.BlockSpec((tm, D), lambda i, f: (i, 0)),
        out_shape=jax.ShapeDtypeStruct((T, D), F32),
        scratch_shapes=[pltpu.VMEM((tm, D), BF16)],
        compiler_params=_params("parallel", "arbitrary"),
        name="mlp",
    )(x2, g, w1, w2, gf)


def _odd_prep_kernel(x_ref, g_ref, w_ref, gq_ref, wuq_ref, gkv_ref, wukv_ref, cos_ref, sin_ref,
                     q_ref, k_ref, vt_ref, xr_ref, xg_ref, *, q_scale):
    h = _rms(x_ref[...], g_ref[...]).astype(BF16)
    z = jnp.dot(h, w_ref[...], preferred_element_type=F32)
    cos = cos_ref[...]
    sin = sin_ref[...]
    o1 = C_Q_RANK
    o2 = o1 + C_KV_RANK
    o3 = o2 + LANES
    cq = _rms(z[:, :o1], gq_ref[...]).astype(BF16)
    ckv = _rms(z[:, o1:o2], gkv_ref[...]).astype(BF16)
    kr = _rope_slab(z[:, o2:o3], cos, sin, C_ROPE // 2)
    q = jnp.dot(cq, wuq_ref[...], preferred_element_type=F32)
    kv = jnp.dot(ckv, wukv_ref[...], preferred_element_type=F32)
    lane = lax.broadcasted_iota(jnp.int32, (1, LANES), 1)
    ones_col = (lane == C_VDIM).astype(F32)
    for hd in range(C_HEADS):
        qs = _rope_slab(q[:, hd * LANES:(hd + 1) * LANES], cos, sin, C_ROPE // 2)
        q_ref[hd] = (qs * q_scale).astype(BF16)
        k_ref[hd] = (kv[:, hd * LANES:(hd + 1) * LANES] + kr).astype(BF16)
        vo = (C_HEADS + hd) * LANES
        vt_ref[hd] = (kv[:, vo:vo + LANES] + ones_col).T[:V_ROWS].astype(BF16)
    xr_ref[...] = z[:, o3:o3 + D_WIDTH]
    xg_ref[...] = z[:, o3 + D_WIDTH:]


def _odd_prep(x2, g, w, gq, wuq, gkv, wukv, cos, sin, seq, tm, tk, q_scale):
    T, D = x2.shape
    nt_seq = seq // tm
    per_chunk = tk // tm
    tok = lambda i: (i, 0)
    tab = lambda i: (i % nt_seq, 0)
    fixed = lambda i: (0, 0)
    head = lambda i: (0, i, 0)
    hshape = jax.ShapeDtypeStruct((C_HEADS, T, LANES), BF16)
    return pl.pallas_call(
        functools.partial(_odd_prep_kernel, q_scale=q_scale),
        grid=(T // tm,),
        in_specs=[pl.BlockSpec((tm, D), tok), pl.BlockSpec((1, D), fixed), pl.BlockSpec(w.shape, fixed),
                  pl.BlockSpec((1, C_Q_RANK), fixed), pl.BlockSpec(wuq.shape, fixed),
                  pl.BlockSpec((1, C_KV_RANK), fixed), pl.BlockSpec(wukv.shape, fixed),
                  pl.BlockSpec((tm, LANES), tab), pl.BlockSpec((tm, LANES), tab)],
        out_specs=[pl.BlockSpec((C_HEADS, tm, LANES), head), pl.BlockSpec((C_HEADS, tm, LANES), head),
                   pl.BlockSpec((C_HEADS, None, V_ROWS, tm), lambda i: (0, i // per_chunk, 0, i % per_chunk)),
                   pl.BlockSpec((tm, D_WIDTH), tok), pl.BlockSpec((tm, D_WIDTH), tok)],
        out_shape=[hshape, hshape, jax.ShapeDtypeStruct((C_HEADS, T // tk, V_ROWS, tk), BF16),
                   jax.ShapeDtypeStruct((T, D_WIDTH), F32), jax.ShapeDtypeStruct((T, D_WIDTH), F32)],
        compiler_params=_params("parallel"),
        name="odd_prep",
    )(x2, g, w, gq, wuq, gkv, wukv, cos, sin)


def _flash_kernel(q_ref, k_ref, vt_ref, o_ref, s_ref, p_ref, acc_ref, *, tq, tk):
    seq = k_ref.shape[0]
    n_chunks = seq // tk
    n_steps = (seq // tq) * n_chunks

    def scores(g):
        tile = g // n_chunks
        q = q_ref[pl.ds(pl.multiple_of(tile * tq, tq), tq), :]
        kc = k_ref[pl.ds(pl.multiple_of((g % n_chunks) * tk, tk), tk), :]
        st = lax.dot_general(kc, q, (((1,), (1,)), ((), ())), preferred_element_type=F32)
        s_ref[...] = st
        return jnp.max(st, axis=0, keepdims=True)

    def values(g, alpha):
        acc_ref[...] = alpha * acc_ref[...] + jnp.dot(vt_ref[g % n_chunks], p_ref[...],
                                                       preferred_element_type=F32)

    def finish_tile(tile):
        acc = acc_ref[...]
        out = acc[:C_VDIM] / acc[C_VDIM:C_VDIM + 1, :]
        out = jnp.concatenate([out, jnp.zeros((LANES - C_VDIM, tq), F32)], axis=0)
        o_ref[pl.ds(pl.multiple_of(tile * tq, tq), tq), :] = out.T.astype(BF16)
        acc_ref[...] = jnp.zeros_like(acc_ref)

    p_ref[...] = jnp.zeros_like(p_ref)
    acc_ref[...] = jnp.zeros_like(acc_ref)
    mx0 = scores(0)

    def body(g, carry):
        m, mx, alpha = carry
        first = g % n_chunks == 0
        values(jnp.maximum(g - 1, 0), alpha)
        m_old = jnp.where(first, NEG_BIG, m)
        m_new = jnp.maximum(m_old, mx)
        p_ref[...] = jnp.exp2(s_ref[...] - m_new).astype(BF16)
        mx_next = scores(jnp.minimum(g + 1, n_steps - 1))

        @pl.when(jnp.logical_and(first, g > 0))
        def _():
            finish_tile(g // n_chunks - 1)

        return m_new, mx_next, jnp.exp2(m_old - m_new)

    m0 = jnp.full((1, tq), NEG_BIG, F32)
    _, _, alpha = lax.fori_loop(0, n_steps, body, (m0, mx0, jnp.ones((1, tq), F32)))
    values(n_steps - 1, alpha)
    finish_tile(seq // tq - 1)


def _flash(q, k, vt, batch, seq, tq, tk):
    H, T, _ = q.shape
    n_chunks = seq // tk
    per_head = lambda b, h: (h, b, 0)
    return pl.pallas_call(
        functools.partial(_flash_kernel, tq=tq, tk=tk),
        grid=(batch, H),
        in_specs=[pl.BlockSpec((None, seq, LANES), per_head, pipeline_mode=pl.Buffered(1)),
                  pl.BlockSpec((None, seq, LANES), per_head),
                  pl.BlockSpec((None, n_chunks, V_ROWS, tk), lambda b, h: (h, b, 0, 0))],
        out_specs=pl.BlockSpec((None, seq, LANES), per_head, pipeline_mode=pl.Buffered(1)),
        out_shape=jax.ShapeDtypeStruct((H, T, LANES), BF16),
        scratch_shapes=[pltpu.VMEM((tk, tq), F32), pltpu.VMEM((tk, tq), BF16), pltpu.VMEM((V_ROWS, tq), F32)],
        compiler_params=_params("parallel", "parallel"),
        name="mla_flash",
    )(q, k, vt)


def _scan_tile(a, b, carry, reverse):
    rows, width = a.shape
    groups = rows // SUBLANES
    a = a.reshape(groups, SUBLANES, width)
    b = b.reshape(groups, SUBLANES, width)
    sub = lax.broadcasted_iota(jnp.int32, (1, SUBLANES, 1), 1)
    k = 1
    while k < SUBLANES:
        shift = (SUBLANES - k) if reverse else k
        ok = (sub < SUBLANES - k) if reverse else (sub >= k)
        a_s = pltpu.roll(a, shift, 1)
        b_s = pltpu.roll(b, shift, 1)
        b = jnp.where(ok, a * b_s + b, b)
        a = jnp.where(ok, a * a_s, a)
        k *= 2
    order = range(groups - 1, -1, -1) if reverse else range(groups)
    edge = 0 if reverse else SUBLANES - 1
    out = [None] * groups
    for gi in order:
        hg = a[gi] * carry + b[gi]
        out[gi] = hg
        carry = hg[edge:edge + 1]
    return jnp.concatenate(out, axis=0), carry


def _lru_gates(xp_ref, xc_ref, xn_ref, lo_edge, hi_edge, cw_ref, cb_ref, wg_ref, bg_ref, lam_row, ext_ref):
    tm = xc_ref.shape[0]
    ext_ref[0:SUBLANES, :] = jnp.where(lo_edge, 0.0, xp_ref[...])
    ext_ref[SUBLANES:SUBLANES + tm, :] = xc_ref[...]
    ext_ref[SUBLANES + tm:, :] = jnp.where(hi_edge, 0.0, xn_ref[...])
    left = CONV_WIDTH // 2
    xc = cb_ref[...] + cw_ref[0:1, :] * ext_ref[SUBLANES - left:SUBLANES - left + tm, :]
    for j in range(1, CONV_WIDTH):
        xc = xc + cw_ref[j:j + 1, :] * ext_ref[SUBLANES - left + j:SUBLANES - left + j + tm, :]
    gates = jnp.dot(xc.astype(BF16), wg_ref[...], preferred_element_type=F32) + bg_ref[...]
    gates = 1.0 / (1.0 + jnp.exp(-gates))
    r = gates[:, :D_WIDTH]
    ig = gates[:, D_WIDTH:]
    neg = -lam_row
    softplus = jnp.maximum(neg, 0.0) + jnp.log1p(jnp.exp(-jnp.abs(neg)))
    log_a = (-LRU_C) * r * softplus
    a = jnp.exp(log_a)
    th = jnp.tanh(log_a)
    b = jnp.sqrt(-2.0 * th / (1.0 - th)) * (ig * xc)
    return a, b


def _lru_kernel(fp_ref, fc_ref, fn_ref, bp_ref, bc_ref, bn_ref, cw_ref, cb_ref, wgf_ref, bgf_ref,
                wgb_ref, bgb_ref, lam_ref, hf_ref, hb_ref, ext_ref, carry_ref):
    i = pl.program_id(1)
    last = pl.num_programs(1) - 1

    @pl.when(i == 0)
    def _():
        carry_ref[...] = jnp.zeros_like(carry_ref)

    a, b = _lru_gates(fp_ref, fc_ref, fn_ref, i == 0, i == last, cw_ref, cb_ref, wgf_ref, bgf_ref,
                      lam_ref[0:1, :], ext_ref)
    h, cf = _scan_tile(a, b, carry_ref[0:1, :], reverse=False)
    hf_ref[...] = h
    carry_ref[0:1, :] = cf

    a, b = _lru_gates(bp_ref, bc_ref, bn_ref, i == last, i == 0, cw_ref, cb_ref, wgb_ref, bgb_ref,
                      lam_ref[1:2, :], ext_ref)
    h, cb = _scan_tile(a, b, carry_ref[1:2, :], reverse=True)
    hb_ref[...] = h
    carry_ref[1:2, :] = cb


def _lru(xr, cw, cb, wgf, bgf, wgb, bgb, lam, batch, seq, tm):
    T, W = xr.shape
    nt = seq // tm
    hb = tm // SUBLANES
    n_hb = T // SUBLANES
    f_tok = lambda b, i: (b * nt + i, 0)
    b_tok = lambda b, i: (b * nt + nt - 1 - i, 0)
    f_prev = lambda b, i: (jnp.maximum((b * nt + i) * hb - 1, 0), 0)
    f_next = lambda b, i: (jnp.minimum((b * nt + i + 1) * hb, n_hb - 1), 0)
    b_prev = lambda b, i: (jnp.maximum((b * nt + nt - 1 - i) * hb - 1, 0), 0)
    b_next = lambda b, i: (jnp.minimum((b * nt + nt - i) * hb, n_hb - 1), 0)
    fixed = lambda b, i: (0, 0)
    halo = (SUBLANES, W)
    return pl.pallas_call(
        _lru_kernel,
        grid=(batch, nt),
        in_specs=[pl.BlockSpec(halo, f_prev), pl.BlockSpec((tm, W), f_tok), pl.BlockSpec(halo, f_next),
                  pl.BlockSpec(halo, b_prev), pl.BlockSpec((tm, W), b_tok), pl.BlockSpec(halo, b_next),
                  pl.BlockSpec(cw.shape, fixed), pl.BlockSpec(cb.shape, fixed),
                  pl.BlockSpec(wgf.shape, fixed), pl.BlockSpec(bgf.shape, fixed),
                  pl.BlockSpec(wgb.shape, fixed), pl.BlockSpec(bgb.shape, fixed),
                  pl.BlockSpec(lam.shape, fixed)],
        out_specs=[pl.BlockSpec((tm, W), f_tok), pl.BlockSpec((tm, W), b_tok)],
        out_shape=[jax.ShapeDtypeStruct((T, W), F32), jax.ShapeDtypeStruct((T, W), F32)],
        scratch_shapes=[pltpu.VMEM((tm + 2 * SUBLANES, W), F32), pltpu.VMEM((2, W), F32)],
        compiler_params=_params("arbitrary", "arbitrary"),
        name="rglru",
    )(xr, xr, xr, xr, xr, xr, cw, cb, wgf, bgf, wgb, bgb, lam)


def _gelu_tanh(x):
    return 0.5 * x * (1.0 + jnp.tanh(math.sqrt(2.0 / math.pi) * (x + 0.044715 * (x * x * x))))


def _odd_out_kernel(x_ref, yc_ref, hf_ref, hb_ref, xg_ref, w_ref, o_ref):
    lane = lax.broadcasted_iota(jnp.int32, (1, LANES), 1)
    low = lane < C_VDIM
    parts = []
    for hd in range(0, C_HEADS, 2):
        even = yc_ref[hd].astype(F32)
        odd = pltpu.roll(yc_ref[hd + 1].astype(F32), C_VDIM, 1)
        parts.append(jnp.where(low, even, odd).astype(BF16))
    parts.append(((hf_ref[...] + hb_ref[...]) * _gelu_tanh(xg_ref[...])).astype(BF16))
    ycat = jnp.concatenate(parts, axis=1)
    o_ref[...] = x_ref[...] + jnp.dot(ycat, w_ref[...], preferred_element_type=F32)


def _odd_out(x2, yc, hf, hb, xg, w, tm):
    T, D = x2.shape
    tok = lambda i: (i, 0)
    return pl.pallas_call(
        _odd_out_kernel,
        grid=(T // tm,),
        in_specs=[pl.BlockSpec((tm, D), tok), pl.BlockSpec((C_HEADS, tm, LANES), lambda i: (0, i, 0)),
                  pl.BlockSpec((tm, D_WIDTH), tok), pl.BlockSpec((tm, D_WIDTH), tok), pl.BlockSpec((tm, D_WIDTH), tok),
                  pl.BlockSpec(w.shape, lambda i: (0, 0))],
        out_specs=pl.BlockSpec((tm, D), tok),
        out_shape=jax.ShapeDtypeStruct((T, D), F32),
        compiler_params=_params("parallel"),
        name="odd_out",
    )(x2, yc, hf, hb, xg, w)


def _rope_tables(seq, half, lane_lo, lane_hi):
    inv = ROPE_THETA ** (-jnp.arange(half, dtype=F32) / half)
    ang = jnp.arange(seq, dtype=F32)[:, None] * inv[None, :]
    lane = jnp.arange(LANES)
    idx = lane % half
    sign = jnp.where((lane % (2 * half)) < half, -1.0, 1.0).astype(F32)
    active = (lane >= lane_lo) & (lane < lane_hi)
    cos = jnp.where(active[None, :], jnp.cos(ang)[:, idx], 1.0)
    sin = jnp.where(active[None, :], jnp.sin(ang)[:, idx] * sign[None, :], 0.0)
    return cos, sin


def _tile_of(n, pref):
    t = min(n, pref)
    assert n % t == 0
    return t


def kernel(x, e_norm_mix, e_w_in, e_sink, e_w_pool, e_pool_scale, e_w_out, o_norm_mix, o_w_in, o_g_cq, o_w_uq,
           o_g_ckv, o_w_ukv, o_conv_w, o_conv_b, o_lru_wa, o_lru_ba, o_lru_wx, o_lru_bx, o_lru_lambda, o_w_out,
           norm_mlp, w_mlp1, w_mlp2, final_norm):
    batch, seq, D = x.shape
    T = batch * seq
    depth = norm_mlp.shape[0]
    x2 = x.reshape(T, D)

    tm_prep = _tile_of(seq, 512)
    tq_even = _tile_of(seq, 256)
    tm_mlp = _tile_of(T, 1024)
    tf_mlp = _tile_of(w_mlp1.shape[2], 1024)
    tq_flash = _tile_of(seq, 512)
    tk_flash = _tile_of(seq, 8192)
    tm_lru = _tile_of(seq, 256)
    tm_out = _tile_of(seq, 512)

    cos_a, sin_a = _rope_tables(seq, A_HEAD_DIM // 2, 0, LANES)
    cos_c, sin_c = _rope_tables(seq, C_ROPE // 2, C_NOPE, C_NOPE + C_ROPE)
    q_scale_c = (C_NOPE + C_ROPE) ** -0.5 * LOG2E

    hd = A_HEAD_DIM
    kv_heads = [e_w_in[:, :, A_WIDTH + j * hd:A_WIDTH + (j + 1) * hd] for j in range(2 * A_KV_HEADS)]
    e_w_in_p = jnp.concatenate([e_w_in[:, :, :A_WIDTH]] + [w for w in kv_heads for _ in range(2)]
                               + [e_w_in[:, :, A_WIDTH + 2 * A_KV_HEADS * hd:]],
                               axis=2).astype(BF16)
    i2 = C_Q_RANK + C_KV_RANK
    i3 = i2 + C_ROPE
    kr_slab = jnp.pad(o_w_in[:, :, i2:i3], ((0, 0), (0, 0), (C_NOPE, LANES - C_NOPE - C_ROPE)))
    o_w_in_p = jnp.concatenate([o_w_in[:, :, :i2], kr_slab, o_w_in[:, :, i3:]], axis=2).astype(BF16)
    n_odd = o_w_in.shape[0]
    wuq = o_w_uq.reshape(n_odd, C_Q_RANK, C_HEADS, C_NOPE + C_ROPE)
    wuq = jnp.pad(wuq, ((0, 0), (0, 0), (0, 0), (0, LANES - C_NOPE - C_ROPE)))
    wuq = wuq.reshape(n_odd, C_Q_RANK, C_HEADS * LANES).astype(BF16)
    wukv = o_w_ukv.reshape(n_odd, C_KV_RANK, C_HEADS, C_NOPE + C_VDIM)
    lane_pad = ((0, 0), (0, 0), (0, 0), (0, LANES - C_NOPE))
    wukv = jnp.concatenate([jnp.pad(wukv[..., :C_NOPE], lane_pad).reshape(n_odd, C_KV_RANK, -1),
                            jnp.pad(wukv[..., C_NOPE:], lane_pad).reshape(n_odd, C_KV_RANK, -1)],
                           axis=2).astype(BF16)
    eye = jnp.eye(D_BLOCKS, dtype=F32)[None, None, :, None, :, None]
    dense = lambda w: (eye * w[:, :, :, :, None, :]).reshape(n_odd, 2, D_WIDTH, D_WIDTH)
    gate_w = jnp.concatenate([dense(o_lru_wa), dense(o_lru_wx)], axis=3).astype(BF16)
    gate_b = jnp.concatenate([o_lru_ba, o_lru_bx], axis=2)[:, :, None, :]
    e_w_pool_p = e_w_pool.astype(BF16)
    e_w_out_p = e_w_out.astype(BF16)
    o_w_out_p = o_w_out.astype(BF16)
    w1_p = w_mlp1.astype(BF16)
    w2_p = w_mlp2.astype(BF16)
    row = lambda g: g.reshape(g.shape[0], 1, g.shape[1])
    e_norm, o_norm, mlp_norm = row(e_norm_mix), row(o_norm_mix), row(norm_mlp)
    g_cq, g_ckv, conv_b, pool_scale = row(o_g_cq), row(o_g_ckv), row(o_conv_b), row(e_pool_scale)
    gf = final_norm.reshape(1, D)

    for layer in range(depth):
        li = layer // 2
        if layer % 2 == 0:
            q, k, v, u = _even_prep(x2, e_norm[li], e_w_in_p[li], cos_a, sin_a, seq, tm_prep)
            x2 = _even_mix(x2, q, k, v, u, e_sink[li], e_w_pool_p[li], pool_scale[li], e_w_out_p[li], seq, tq_even)
        else:
            q, k, vt, xr, xg = _odd_prep(x2, o_norm[li], o_w_in_p[li], g_cq[li], wuq[li], g_ckv[li], wukv[li],
                                         cos_c, sin_c, seq, tm_prep, tk_flash, q_scale_c)
            yc = _flash(q, k, vt, batch, seq, tq_flash, tk_flash)
            hf, hb = _lru(xr, o_conv_w[li], conv_b[li], gate_w[li, 0], gate_b[li, 0], gate_w[li, 1], gate_b[li, 1],
                          o_lru_lambda[li], batch, seq, tm_lru)
            x2 = _odd_out(x2, yc, hf, hb, xg, o_w_out_p[li], tm_out)
        x2 = _mlp(x2, mlp_norm[layer], w1_p[layer], w2_p[layer], gf, tm_mlp, tf_mlp, final_norm=(layer == depth - 1))
    return x2.reshape(batch, seq, D)
```

```python
import functools
import math

import jax
import jax.numpy as jnp
from jax import lax
from jax.experimental import pallas as pl
from jax.experimental.pallas import tpu as pltpu

F32 = jnp.float32
BF16 = jnp.bfloat16

EPS = 1e-6
ROPE_THETA = 10000.0
LANES = 128
SUBLANES = 8
VMEM_LIMIT = 48 * 1024 * 1024

A_HEADS = 8
A_KV_HEADS = 2
A_HEAD_DIM = 64
A_WINDOW = 128
A_WIDTH = A_HEADS * A_HEAD_DIM
POOL_WINDOWS = (2, 4, 8, 16)
B_GROUP_DIM = 128
B_WIDTH = B_GROUP_DIM * len(POOL_WINDOWS)
POOL_HALO = 8
C_HEADS = 8
C_NOPE = 64
C_ROPE = 32
C_VDIM = 64
C_Q_RANK = 256
C_KV_RANK = 128
V_ROWS = 80
D_WIDTH = 512
D_BLOCKS = 8
CONV_WIDTH = 4
LRU_C = 8.0

NEG_BIG = -1e30
LOG2E = math.log2(math.e)


def _params(*sem):
    return pltpu.CompilerParams(dimension_semantics=sem, vmem_limit_bytes=VMEM_LIMIT)


def _rms(x, g):
    return x * lax.rsqrt(jnp.mean(x * x, axis=-1, keepdims=True) + EPS) * g


def _rope_slab(xs, cos, sin, half):
    lane = lax.broadcasted_iota(jnp.int32, (1, LANES), 1)
    first = (lane % (2 * half)) < half
    swapped = jnp.where(first, pltpu.roll(xs, LANES - half, 1), pltpu.roll(xs, half, 1))
    return xs * cos + swapped * sin


def _even_prep_kernel(x_ref, g_ref, w_ref, cos_ref, sin_ref, q_ref, k_ref, v_ref, u_ref):
    h = _rms(x_ref[...], g_ref[...]).astype(BF16)
    z = jnp.dot(h, w_ref[...], preferred_element_type=F32)
    cos = cos_ref[...]
    sin = sin_ref[...]
    n_q = A_WIDTH // LANES
    for c in range(n_q):
        r = _rope_slab(z[:, c * LANES:(c + 1) * LANES], cos, sin, A_HEAD_DIM // 2)
        q_ref[:, c * LANES:(c + 1) * LANES] = (r * (A_HEAD_DIM ** -0.5 * LOG2E)).astype(BF16)
    for c in range(A_KV_HEADS):
        lo = A_WIDTH + c * LANES
        k_ref[:, c * LANES:(c + 1) * LANES] = _rope_slab(z[:, lo:lo + LANES], cos, sin, A_HEAD_DIM // 2).astype(BF16)
    lo = A_WIDTH + A_KV_HEADS * LANES
    v_ref[...] = z[:, lo:lo + A_KV_HEADS * LANES].astype(BF16)
    u_ref[...] = z[:, lo + A_KV_HEADS * LANES:]


def _even_prep(x2, g, w, cos, sin, seq, tm):
    T, D = x2.shape
    n_in = w.shape[1]
    kvw = A_KV_HEADS * LANES
    nt_seq = seq // tm
    tok = lambda i: (i, 0)
    tab = lambda i: (i % nt_seq, 0)
    fixed = lambda i: (0, 0)
    return pl.pallas_call(
        _even_prep_kernel,
        grid=(T // tm,),
        in_specs=[pl.BlockSpec((tm, D), tok), pl.BlockSpec((1, D), fixed), pl.BlockSpec((D, n_in), fixed),
                  pl.BlockSpec((tm, LANES), tab), pl.BlockSpec((tm, LANES), tab)],
        out_specs=[pl.BlockSpec((tm, A_WIDTH), tok), pl.BlockSpec((tm, kvw), tok),
                   pl.BlockSpec((tm, kvw), tok), pl.BlockSpec((tm, B_WIDTH), tok)],
        out_shape=[jax.ShapeDtypeStruct((T, A_WIDTH), BF16), jax.ShapeDtypeStruct((T, kvw), BF16),
                   jax.ShapeDtypeStruct((T, kvw), BF16), jax.ShapeDtypeStruct((T, B_WIDTH), F32)],
        compiler_params=_params("parallel"),
        name="even_prep",
    )(x2, g, w, cos, sin)


def _even_mix_kernel(sink_ref, x_ref, q_ref, kp_ref, kc_ref, kn_ref, vp_ref, vc_ref, vn_ref,
                     up_ref, uc_ref, un_ref, wpool_ref, pscale_ref, wout_ref, o_ref, uext_ref, *, seq, tq):
    i = pl.program_id(0)
    nt_seq = seq // tq
    pos0 = (i % nt_seq) * tq
    at_start = pos0 == 0
    at_end = pos0 + tq == seq
    nk = tq + 2 * A_WINDOW

    kd = jnp.concatenate([kp_ref[...], kc_ref[...], kn_ref[...]], axis=0)
    vd = jnp.concatenate([vp_ref[...], vc_ref[...], vn_ref[...]], axis=0)
    r = lax.broadcasted_iota(jnp.int32, (tq, nk), 0)
    c = lax.broadcasted_iota(jnp.int32, (tq, nk), 1)
    kpos = pos0 - A_WINDOW + c
    valid = (jnp.abs(c - A_WINDOW - r) <= A_WINDOW) & (kpos >= 0) & (kpos < seq)
    valid2 = jnp.concatenate([valid, valid], axis=0)
    lane = lax.broadcasted_iota(jnp.int32, (1, LANES), 1)
    low = lane < A_HEAD_DIM
    row2 = lax.broadcasted_iota(jnp.int32, (2 * tq, 1), 0)
    ya = []
    for j in range(A_HEADS // 2):
        g = (2 * j) // (A_HEADS // A_KV_HEADS)
        qp = q_ref[:, j * LANES:(j + 1) * LANES]
        zero = jnp.zeros_like(qp)
        lhs = jnp.concatenate([jnp.where(low, qp, zero), jnp.where(low, zero, qp)], axis=0)
        s = lax.dot_general(lhs, kd[:, g * LANES:(g + 1) * LANES], (((1,), (1,)), ((), ())),
                            preferred_element_type=F32)
        s = jnp.where(valid2, s, NEG_BIG)
        sk = jnp.where(row2 < tq, sink_ref[2 * j], sink_ref[2 * j + 1]) * LOG2E
        m = jnp.maximum(jnp.max(s, axis=1, keepdims=True), sk)
        p = jnp.exp2(s - m)
        den = jnp.sum(p, axis=1, keepdims=True) + jnp.exp2(sk - m)
        o = jnp.dot(p.astype(BF16), vd[:, g * LANES:(g + 1) * LANES], preferred_element_type=F32) / den
        ya.append(jnp.where(low, o[:tq], o[tq:]))

    uext_ref[0:POOL_HALO, :] = jnp.where(at_start, 0.0, up_ref[...])
    uext_ref[POOL_HALO:POOL_HALO + tq, :] = uc_ref[...]
    uext_ref[POOL_HALO + tq:, :] = jnp.where(at_end, 0.0, un_ref[...])
    t = pos0 + lax.broadcasted_iota(jnp.int32, (tq, 1), 0)
    yb = []
    for gi, w in enumerate(POOL_WINDOWS):
        half = w // 2
        cols = slice(gi * B_GROUP_DIM, (gi + 1) * B_GROUP_DIM)
        win = uext_ref[POOL_HALO - half:POOL_HALO - half + tq, cols]
        for off in range(-half + 1, half):
            win = win + uext_ref[POOL_HALO + off:POOL_HALO + off + tq, cols]
        cnt = (jnp.minimum(t + half, seq) - jnp.maximum(t - half, 0)).astype(F32)
        d = win / cnt - uc_ref[:, cols]
        y = jnp.dot(d.astype(BF16), wpool_ref[gi], preferred_element_type=F32)
        yb.append(y * pscale_ref[:, cols])

    ycat = jnp.concatenate(ya + yb, axis=1).astype(BF16)
    o_ref[...] = x_ref[...] + jnp.dot(ycat, wout_ref[...], preferred_element_type=F32)


def _even_mix(x2, q, k, v, u, sink, wpool, pscale, wout, seq, tq):
    T, D = x2.shape
    kvw = k.shape[1]
    nt_seq = seq // tq
    wb = tq // A_WINDOW
    hb = tq // POOL_HALO
    n_wb = T // A_WINDOW
    n_hb = T // POOL_HALO
    tok = lambda i: (i, 0)
    prev_w = lambda i: (jnp.maximum(i * wb - 1, 0), 0)
    next_w = lambda i: (jnp.minimum((i + 1) * wb, n_wb - 1), 0)
    prev_h = lambda i: (jnp.maximum(i * hb - 1, 0), 0)
    next_h = lambda i: (jnp.minimum((i + 1) * hb, n_hb - 1), 0)
    fixed2 = lambda i: (0, 0)
    fixed3 = lambda i: (0, 0, 0)
    return pl.pallas_call(
        functools.partial(_even_mix_kernel, seq=seq, tq=tq),
        grid=(T // tq,),
        in_specs=[pl.BlockSpec(memory_space=pltpu.SMEM),
                  pl.BlockSpec((tq, D), tok),
                  pl.BlockSpec((tq, A_WIDTH), tok),
                  pl.BlockSpec((A_WINDOW, kvw), prev_w), pl.BlockSpec((tq, kvw), tok), pl.BlockSpec((A_WINDOW, kvw), next_w),
                  pl.BlockSpec((A_WINDOW, kvw), prev_w), pl.BlockSpec((tq, kvw), tok), pl.BlockSpec((A_WINDOW, kvw), next_w),
                  pl.BlockSpec((POOL_HALO, B_WIDTH), prev_h), pl.BlockSpec((tq, B_WIDTH), tok), pl.BlockSpec((POOL_HALO, B_WIDTH), next_h),
                  pl.BlockSpec(wpool.shape, fixed3), pl.BlockSpec((1, B_WIDTH), fixed2), pl.BlockSpec((D, D), fixed2)],
        out_specs=pl.BlockSpec((tq, D), tok),
        out_shape=jax.ShapeDtypeStruct((T, D), F32),
        scratch_shapes=[pltpu.VMEM((tq + 2 * POOL_HALO, B_WIDTH), F32)],
        compiler_params=_params("parallel"),
        name="even_mix",
    )(sink, x2, q, k, k, k, v, v, v, u, u, u, wpool, pscale, wout)


def _mlp_kernel(x_ref, g_ref, w1_ref, w2_ref, gf_ref, o_ref, h_ref, *, final_norm):
    f = pl.program_id(1)

    @pl.when(f == 0)
    def _():
        x = x_ref[...]
        h_ref[...] = _rms(x, g_ref[...]).astype(BF16)
        o_ref[...] = x

    u = jnp.maximum(jnp.dot(h_ref[...], w1_ref[...], preferred_element_type=F32), 0.0)
    o_ref[...] += jnp.dot((u * u).astype(BF16), w2_ref[...], preferred_element_type=F32)

    if final_norm:
        @pl.when(f == pl.num_programs(1) - 1)
        def _():
            o_ref[...] = _rms(o_ref[...], gf_ref[...])


def _mlp(x2, g, w1, w2, gf, tm, tf, final_norm):
    T, D = x2.shape
    F = w1.shape[1]
    return pl.pallas_call(
        functools.partial(_mlp_kernel, final_norm=final_norm),
        grid=(T // tm, F // tf),
        in_specs=[pl.BlockSpec((tm, D), lambda i, f: (i, 0)), pl.BlockSpec((1, D), lambda i, f: (0, 0)),
                  pl.BlockSpec((D, tf), lambda i, f: (0, f)), pl.BlockSpec((tf, D), lambda i, f: (f, 0)),
                  pl.BlockSpec((1, D), lambda i, f: (0, 0))],
        out_specs=pl.BlockSpec((tm, D), lambda i, f: (i, 0)),
        out_shape=jax.ShapeDtypeStruct((T, D), F32),
        scratch_shapes=[pltpu.VMEM((tm, D), BF16)],
        compiler_params=_params("parallel", "arbitrary"),
        name="mlp",
    )(x2, g, w1, w2, gf)


def _odd_prep_kernel(x_ref, g_ref, w_ref, gq_ref, wuq_ref, gkv_ref, wukv_ref, cos_ref, sin_ref,
                     q_ref, k_ref, vt_ref, xr_ref, xg_ref, *, q_scale):
    h = _rms(x_ref[...], g_ref[...]).astype(BF16)
    z = jnp.dot(h, w_ref[...], preferred_element_type=F32)
    cos = cos_ref[...]
    sin = sin_ref[...]
    o1 = C_Q_RANK
    o2 = o1 + C_KV_RANK
    o3 = o2 + LANES
    cq = _rms(z[:, :o1], gq_ref[...]).astype(BF16)
    ckv = _rms(z[:, o1:o2], gkv_ref[...]).astype(BF16)
    kr = _rope_slab(z[:, o2:o3], cos, sin, C_ROPE // 2)
    q = jnp.dot(cq, wuq_ref[...], preferred_element_type=F32)
    kv = jnp.dot(ckv, wukv_ref[...], preferred_element_type=F32)
    lane = lax.broadcasted_iota(jnp.int32, (1, LANES), 1)
    ones_col = (lane == C_VDIM).astype(F32)
    for hd in range(C_HEADS):
        qs = _rope_slab(q[:, hd * LANES:(hd + 1) * LANES], cos, sin, C_ROPE // 2)
        q_ref[hd] = (qs * q_scale).astype(BF16)
        k_ref[hd] = (kv[:, hd * LANES:(hd + 1) * LANES] + kr).astype(BF16)
        vo = (C_HEADS + hd) * LANES
        vt_ref[hd] = (kv[:, vo:vo + LANES] + ones_col).T[:V_ROWS].astype(BF16)
    xr_ref[...] = z[:, o3:o3 + D_WIDTH]
    xg_ref[...] = z[:, o3 + D_WIDTH:]


def _odd_prep(x2, g, w, gq, wuq, gkv, wukv, cos, sin, seq, tm, tk, q_scale):
    T, D = x2.shape
    nt_seq = seq // tm
    per_chunk = tk // tm
    tok = lambda i: (i, 0)
    tab = lambda i: (i % nt_seq, 0)
    fixed = lambda i: (0, 0)
    head = lambda i: (0, i, 0)
    hshape = jax.ShapeDtypeStruct((C_HEADS, T, LANES), BF16)
    return pl.pallas_call(
        functools.partial(_odd_prep_kernel, q_scale=q_scale),
        grid=(T // tm,),
        in_specs=[pl.BlockSpec((tm, D), tok), pl.BlockSpec((1, D), fixed), pl.BlockSpec(w.shape, fixed),
                  pl.BlockSpec((1, C_Q_RANK), fixed), pl.BlockSpec(wuq.shape, fixed),
                  pl.BlockSpec((1, C_KV_RANK), fixed), pl.BlockSpec(wukv.shape, fixed),
                  pl.BlockSpec((tm, LANES), tab), pl.BlockSpec((tm, LANES), tab)],
        out_specs=[pl.BlockSpec((C_HEADS, tm, LANES), head), pl.BlockSpec((C_HEADS, tm, LANES), head),
                   pl.BlockSpec((C_HEADS, None, V_ROWS, tm), lambda i: (0, i // per_chunk, 0, i % per_chunk)),
                   pl.BlockSpec((tm, D_WIDTH), tok), pl.BlockSpec((tm, D_WIDTH), tok)],
        out_shape=[hshape, hshape, jax.ShapeDtypeStruct((C_HEADS, T // tk, V_ROWS, tk), BF16),
                   jax.ShapeDtypeStruct((T, D_WIDTH), F32), jax.ShapeDtypeStruct((T, D_WIDTH), F32)],
        compiler_params=_params("parallel"),
        name="odd_prep",
    )(x2, g, w, gq, wuq, gkv, wukv, cos, sin)


def _flash_kernel(q_ref, k_ref, vt_ref, o_ref, s_ref, p_ref, acc_ref, *, tq, tk):
    seq = k_ref.shape[0]
    n_chunks = seq // tk
    n_steps = (seq // tq) * n_chunks

    def scores(g):
        tile = g // n_chunks
        q = q_ref[pl.ds(pl.multiple_of(tile * tq, tq), tq), :]
        kc = k_ref[pl.ds(pl.multiple_of((g % n_chunks) * tk, tk), tk), :]
        st = lax.dot_general(kc, q, (((1,), (1,)), ((), ())), preferred_element_type=F32)
        s_ref[...] = st
        return jnp.max(st, axis=0, keepdims=True)

    def values(g, alpha):
        acc_ref[...] = alpha * acc_ref[...] + jnp.dot(vt_ref[g % n_chunks], p_ref[...],
                                                       preferred_element_type=F32)

    def finish_tile(tile):
        acc = acc_ref[...]
        out = acc[:C_VDIM] / acc[C_VDIM:C_VDIM + 1, :]
        out = jnp.concatenate([out, jnp.zeros((LANES - C_VDIM, tq), F32)], axis=0)
        o_ref[pl.ds(pl.multiple_of(tile * tq, tq), tq), :] = out.T.astype(BF16)
        acc_ref[...] = jnp.zeros_like(acc_ref)

    p_ref[...] = jnp.zeros_like(p_ref)
    acc_ref[...] = jnp.zeros_like(acc_ref)
    mx0 = scores(0)

    def body(g, carry):
        m, mx, alpha = carry
        first = g % n_chunks == 0
        values(jnp.maximum(g - 1, 0), alpha)
        m_old = jnp.where(first, NEG_BIG, m)
        m_new = jnp.maximum(m_old, mx)
        p_ref[...] = jnp.exp2(s_ref[...] - m_new).astype(BF16)
        mx_next = scores(jnp.minimum(g + 1, n_steps - 1))

        @pl.when(jnp.logical_and(first, g > 0))
        def _():
            finish_tile(g // n_chunks - 1)

        return m_new, mx_next, jnp.exp2(m_old - m_new)

    m0 = jnp.full((1, tq), NEG_BIG, F32)
    _, _, alpha = lax.fori_loop(0, n_steps, body, (m0, mx0, jnp.ones((1, tq), F32)))
    values(n_steps - 1, alpha)
    finish_tile(seq // tq - 1)


def _flash(q, k, vt, batch, seq, tq, tk):
    H, T, _ = q.shape
    n_chunks = seq // tk
    per_head = lambda b, h: (h, b, 0)
    return pl.pallas_call(
        functools.partial(_flash_kernel, tq=tq, tk=tk),
        grid=(batch, H),
        in_specs=[pl.BlockSpec((None, seq, LANES), per_head, pipeline_mode=pl.Buffered(1)),
                  pl.BlockSpec((None, seq, LANES), per_head),
                  pl.BlockSpec((None, n_chunks, V_ROWS, tk), lambda b, h: (h, b, 0, 0))],
        out_specs=pl.BlockSpec((None, seq, LANES), per_head, pipeline_mode=pl.Buffered(1)),
        out_shape=jax.ShapeDtypeStruct((H, T, LANES), BF16),
        scratch_shapes=[pltpu.VMEM((tk, tq), F32), pltpu.VMEM((tk, tq), BF16), pltpu.VMEM((V_ROWS, tq), F32)],
        compiler_params=_params("parallel", "parallel"),
        name="mla_flash",
    )(q, k, vt)


def _scan_tile(a, b, carry, reverse):
    rows, width = a.shape
    groups = rows // SUBLANES
    a = a.reshape(groups, SUBLANES, width)
    b = b.reshape(groups, SUBLANES, width)
    sub = lax.broadcasted_iota(jnp.int32, (1, SUBLANES, 1), 1)
    k = 1
    while k < SUBLANES:
        shift = (SUBLANES - k) if reverse else k
        ok = (sub < SUBLANES - k) if reverse else (sub >= k)
        a_s = pltpu.roll(a, shift, 1)
        b_s = pltpu.roll(b, shift, 1)
        b = jnp.where(ok, a * b_s + b, b)
        a = jnp.where(ok, a * a_s, a)
        k *= 2
    order = range(groups - 1, -1, -1) if reverse else range(groups)
    edge = 0 if reverse else SUBLANES - 1
    out = [None] * groups
    for gi in order:
        hg = a[gi] * carry + b[gi]
        out[gi] = hg
        carry = hg[edge:edge + 1]
    return jnp.concatenate(out, axis=0), carry


def _lru_gates(xp_ref, xc_ref, xn_ref, lo_edge, hi_edge, cw_ref, cb_ref, wg_ref, bg_ref, lam_row, ext_ref):
    tm = xc_ref.shape[0]
    ext_ref[0:SUBLANES, :] = jnp.where(lo_edge, 0.0, xp_ref[...])
    ext_ref[SUBLANES:SUBLANES + tm, :] = xc_ref[...]
    ext_ref[SUBLANES + tm:, :] = jnp.where(hi_edge, 0.0, xn_ref[...])
    left = CONV_WIDTH // 2
    xc = cb_ref[...] + cw_ref[0:1, :] * ext_ref[SUBLANES - left:SUBLANES - left + tm, :]
    for j in range(1, CONV_WIDTH):
        xc = xc + cw_ref[j:j + 1, :] * ext_ref[SUBLANES - left + j:SUBLANES - left + j + tm, :]
    gates = jnp.dot(xc.astype(BF16), wg_ref[...], preferred_element_type=F32) + bg_ref[...]
    gates = 1.0 / (1.0 + jnp.exp(-gates))
    r = gates[:, :D_WIDTH]
    ig = gates[:, D_WIDTH:]
    neg = -lam_row
    softplus = jnp.maximum(neg, 0.0) + jnp.log1p(jnp.exp(-jnp.abs(neg)))
    log_a = (-LRU_C) * r * softplus
    a = jnp.exp(log_a)
    th = jnp.tanh(log_a)
    b = jnp.sqrt(-2.0 * th / (1.0 - th)) * (ig * xc)
    return a, b


def _lru_kernel(fp_ref, fc_ref, fn_ref, bp_ref, bc_ref, bn_ref, cw_ref, cb_ref, wgf_ref, bgf_ref,
                wgb_ref, bgb_ref, lam_ref, hf_ref, hb_ref, ext_ref, carry_ref):
    i = pl.program_id(1)
    last = pl.num_programs(1) - 1

    @pl.when(i == 0)
    def _():
        carry_ref[...] = jnp.zeros_like(carry_ref)

    a, b = _lru_gates(fp_ref, fc_ref, fn_ref, i == 0, i == last, cw_ref, cb_ref, wgf_ref, bgf_ref,
                      lam_ref[0:1, :], ext_ref)
    h, cf = _scan_tile(a, b, carry_ref[0:1, :], reverse=False)
    hf_ref[...] = h
    carry_ref[0:1, :] = cf

    a, b = _lru_gates(bp_ref, bc_ref, bn_ref, i == last, i == 0, cw_ref, cb_ref, wgb_ref, bgb_ref,
                      lam_ref[1:2, :], ext_ref)
    h, cb = _scan_tile(a, b, carry_ref[1:2, :], reverse=True)
    hb_ref[...] = h
    carry_ref[1:2, :] = cb


def _lru(xr, cw, cb, wgf, bgf, wgb, bgb, lam, batch, seq, tm):
    T, W = xr.shape
    nt = seq // tm
    hb = tm // SUBLANES
    n_hb = T // SUBLANES
    f_tok = lambda b, i: (b * nt + i, 0)
    b_tok = lambda b, i: (b * nt + nt - 1 - i, 0)
    f_prev = lambda b, i: (jnp.maximum((b * nt + i) * hb - 1, 0), 0)
    f_next = lambda b, i: (jnp.minimum((b * nt + i + 1) * hb, n_hb - 1), 0)
    b_prev = lambda b, i: (jnp.maximum((b * nt + nt - 1 - i) * hb - 1, 0), 0)
    b_next = lambda b, i: (jnp.minimum((b * nt + nt - i) * hb, n_hb - 1), 0)
    fixed = lambda b, i: (0, 0)
    halo = (SUBLANES, W)
    return pl.pallas_call(
        _lru_kernel,
        grid=(batch, nt),
        in_specs=[pl.BlockSpec(halo, f_prev), pl.BlockSpec((tm, W), f_tok), pl.BlockSpec(halo, f_next),
                  pl.BlockSpec(halo, b_prev), pl.BlockSpec((tm, W), b_tok), pl.BlockSpec(halo, b_next),
                  pl.BlockSpec(cw.shape, fixed), pl.BlockSpec(cb.shape, fixed),
                  pl.BlockSpec(wgf.shape, fixed), pl.BlockSpec(bgf.shape, fixed),
                  pl.BlockSpec(wgb.shape, fixed), pl.BlockSpec(bgb.shape, fixed),
                  pl.BlockSpec(lam.shape, fixed)],
        out_specs=[pl.BlockSpec((tm, W), f_tok), pl.BlockSpec((tm, W), b_tok)],
        out_shape=[jax.ShapeDtypeStruct((T, W), F32), jax.ShapeDtypeStruct((T, W), F32)],
        scratch_shapes=[pltpu.VMEM((tm + 2 * SUBLANES, W), F32), pltpu.VMEM((2, W), F32)],
        compiler_params=_params("arbitrary", "arbitrary"),
        name="rglru",
    )(xr, xr, xr, xr, xr, xr, cw, cb, wgf, bgf, wgb, bgb, lam)


def _gelu_tanh(x):
    return 0.5 * x * (1.0 + jnp.tanh(math.sqrt(2.0 / math.pi) * (x + 0.044715 * (x * x * x))))


def _odd_out_kernel(x_ref, yc_ref, hf_ref, hb_ref, xg_ref, w_ref, o_ref):
    lane = lax.broadcasted_iota(jnp.int32, (1, LANES), 1)
    low = lane < C_VDIM
    parts = []
    for hd in range(0, C_HEADS, 2):
        even = yc_ref[hd].astype(F32)
        odd = pltpu.roll(yc_ref[hd + 1].astype(F32), C_VDIM, 1)
        parts.append(jnp.where(low, even, odd).astype(BF16))
    parts.append(((hf_ref[...] + hb_ref[...]) * _gelu_tanh(xg_ref[...])).astype(BF16))
    ycat = jnp.concatenate(parts, axis=1)
    o_ref[...] = x_ref[...] + jnp.dot(ycat, w_ref[...], preferred_element_type=F32)


def _odd_out(x2, yc, hf, hb, xg, w, tm):
    T, D = x2.shape
    tok = lambda i: (i, 0)
    return pl.pallas_call(
        _odd_out_kernel,
        grid=(T // tm,),
        in_specs=[pl.BlockSpec((tm, D), tok), pl.BlockSpec((C_HEADS, tm, LANES), lambda i: (0, i, 0)),
                  pl.BlockSpec((tm, D_WIDTH), tok), pl.BlockSpec((tm, D_WIDTH), tok), pl.BlockSpec((tm, D_WIDTH), tok),
                  pl.BlockSpec(w.shape, lambda i: (0, 0))],
        out_specs=pl.BlockSpec((tm, D), tok),
        out_shape=jax.ShapeDtypeStruct((T, D), F32),
        compiler_params=_params("parallel"),
        name="odd_out",
    )(x2, yc, hf, hb, xg, w)


def _rope_tables(seq, half, lane_lo, lane_hi):
    inv = ROPE_THETA ** (-jnp.arange(half, dtype=F32) / half)
    ang = jnp.arange(seq, dtype=F32)[:, None] * inv[None, :]
    lane = jnp.arange(LANES)
    idx = lane % half
    sign = jnp.where((lane % (2 * half)) < half, -1.0, 1.0).astype(F32)
    active = (lane >= lane_lo) & (lane < lane_hi)
    cos = jnp.where(active[None, :], jnp.cos(ang)[:, idx], 1.0)
    sin = jnp.where(active[None, :], jnp.sin(ang)[:, idx] * sign[None, :], 0.0)
    return cos, sin


def _tile_of(n, pref):
    t = min(n, pref)
    assert n % t == 0
    return t


def kernel(x, e_norm_mix, e_w_in, e_sink, e_w_pool, e_pool_scale, e_w_out, o_norm_mix, o_w_in, o_g_cq, o_w_uq,
           o_g_ckv, o_w_ukv, o_conv_w, o_conv_b, o_lru_wa, o_lru_ba, o_lru_wx, o_lru_bx, o_lru_lambda, o_w_out,
           norm_mlp, w_mlp1, w_mlp2, final_norm):
    batch, seq, D = x.shape
    T = batch * seq
    depth = norm_mlp.shape[0]
    x2 = x.reshape(T, D)

    tm_prep = _tile_of(seq, 1024)
    tq_even = _tile_of(seq, 256)
    tm_mlp = _tile_of(T, 1024)
    tf_mlp = _tile_of(w_mlp1.shape[2], 1024)
    tq_flash = _tile_of(seq, 512)
    tk_flash = _tile_of(seq, 8192)
    tm_lru = _tile_of(seq, 512)
    tm_out = _tile_of(seq, 1024)

    cos_a, sin_a = _rope_tables(seq, A_HEAD_DIM // 2, 0, LANES)
    cos_c, sin_c = _rope_tables(seq, C_ROPE // 2, C_NOPE, C_NOPE + C_ROPE)
    q_scale_c = (C_NOPE + C_ROPE) ** -0.5 * LOG2E

    hd = A_HEAD_DIM
    kv_heads = [e_w_in[:, :, A_WIDTH + j * hd:A_WIDTH + (j + 1) * hd] for j in range(2 * A_KV_HEADS)]
    e_w_in_p = jnp.concatenate([e_w_in[:, :, :A_WIDTH]] + [w for w in kv_heads for _ in range(2)]
                               + [e_w_in[:, :, A_WIDTH + 2 * A_KV_HEADS * hd:]],
                               axis=2).astype(BF16)
    i2 = C_Q_RANK + C_KV_RANK
    i3 = i2 + C_ROPE
    kr_slab = jnp.pad(o_w_in[:, :, i2:i3], ((0, 0), (0, 0), (C_NOPE, LANES - C_NOPE - C_ROPE)))
    o_w_in_p = jnp.concatenate([o_w_in[:, :, :i2], kr_slab, o_w_in[:, :, i3:]], axis=2).astype(BF16)
    n_odd = o_w_in.shape[0]
    wuq = o_w_uq.reshape(n_odd, C_Q_RANK, C_HEADS, C_NOPE + C_ROPE)
    wuq = jnp.pad(wuq, ((0, 0), (0, 0), (0, 0), (0, LANES - C_NOPE - C_ROPE)))
    wuq = wuq.reshape(n_odd, C_Q_RANK, C_HEADS * LANES).astype(BF16)
    wukv = o_w_ukv.reshape(n_odd, C_KV_RANK, C_HEADS, C_NOPE + C_VDIM)
    lane_pad = ((0, 0), (0, 0), (0, 0), (0, LANES - C_NOPE))
    wukv = jnp.concatenate([jnp.pad(wukv[..., :C_NOPE], lane_pad).reshape(n_odd, C_KV_RANK, -1),
                            jnp.pad(wukv[..., C_NOPE:], lane_pad).reshape(n_odd, C_KV_RANK, -1)],
                           axis=2).astype(BF16)
    eye = jnp.eye(D_BLOCKS, dtype=F32)[None, None, :, None, :, None]
    dense = lambda w: (eye * w[:, :, :, :, None, :]).reshape(n_odd, 2, D_WIDTH, D_WIDTH)
    gate_w = jnp.concatenate([dense(o_lru_wa), dense(o_lru_wx)], axis=3).astype(BF16)
    gate_b = jnp.concatenate([o_lru_ba, o_lru_bx], axis=2)[:, :, None, :]
    e_w_pool_p = e_w_pool.astype(BF16)
    e_w_out_p = e_w_out.astype(BF16)
    o_w_out_p = o_w_out.astype(BF16)
    w1_p = w_mlp1.astype(BF16)
    w2_p = w_mlp2.astype(BF16)
    row = lambda g: g.reshape(g.shape[0], 1, g.shape[1])
    e_norm, o_norm, mlp_norm = row(e_norm_mix), row(o_norm_mix), row(norm_mlp)
    g_cq, g_ckv, conv_b, pool_scale = row(o_g_cq), row(o_g_ckv), row(o_conv_b), row(e_pool_scale)
    gf = final_norm.reshape(1, D)

    for layer in range(depth):
        li = layer // 2
        if layer % 2 == 0:
            q, k, v, u = _even_prep(x2, e_norm[li], e_w_in_p[li], cos_a, sin_a, seq, tm_prep)
            x2 = _even_mix(x2, q, k, v, u, e_sink[li], e_w_pool_p[li], pool_scale[li], e_w_out_p[li], seq, tq_even)
        else:
            q, k, vt, xr, xg = _odd_prep(x2, o_norm[li], o_w_in_p[li], g_cq[li], wuq[li], g_ckv[li], wukv[li],
                                         cos_c, sin_c, seq, tm_prep, tk_flash, q_scale_c)
            yc = _flash(q, k, vt, batch, seq, tq_flash, tk_flash)
            hf, hb = _lru(xr, o_conv_w[li], conv_b[li], gate_w[li, 0], gate_b[li, 0], gate_w[li, 1], gate_b[li, 1],
                          o_lru_lambda[li], batch, seq, tm_lru)
            x2 = _odd_out(x2, yc, hf, hb, xg, o_w_out_p[li], tm_out)
        x2 = _mlp(x2, mlp_norm[layer], w1_p[layer], w2_p[layer], gf, tm_mlp, tf_mlp, final_norm=(layer == depth - 1))
    return x2.reshape(batch, seq, D)
```
